```python
import math
import jax, jax.numpy as jnp
from jax import lax
import numpy as np

D_MODEL = 1024
BATCH = 16
SEQ = 2048
DEPTH = 2

N_EVEN = (DEPTH + 1) // 2
N_ODD = DEPTH // 2
EPS = 1e-6
Q_BLOCK = 128

POOL_WINDOWS = (2, 4, 8, 16)
POOL_GROUPS = len(POOL_WINDOWS)
POOL_WIDTH = D_MODEL // 2
POOL_GROUP_DIM = POOL_WIDTH // POOL_GROUPS

MLA_HEADS = 8
MLA_NOPE_DIM = 64
MLA_ROPE_DIM = 32
MLA_QK_DIM = MLA_NOPE_DIM + MLA_ROPE_DIM
MLA_V_DIM = 64
MLA_Q_RANK = 256
MLA_KV_RANK = 256
ROPE_THETA = 10000.0
MLA_WIDTH = MLA_HEADS * MLA_V_DIM
HYB_IN = POOL_WIDTH + MLA_Q_RANK + MLA_KV_RANK + MLA_ROPE_DIM
HYB_OUT = POOL_WIDTH + MLA_WIDTH

DIFF_HEAD_DIM = 64
DIFF_HEADS = D_MODEL // (2 * DIFF_HEAD_DIM)
DIFF_V_DIM = 2 * DIFF_HEAD_DIM
DIFF_WIDTH = DIFF_HEADS * DIFF_V_DIM
DIFF_QK_WIDTH = DIFF_HEADS * 2 * DIFF_HEAD_DIM

D_FF = ((8 * D_MODEL + 3 * 256 - 1) // (3 * 256)) * 256

kernel_name = "hybrid_pool_mla_diffattn_encoder"


def rmsnorm(x, g):
    xf = x.astype(jnp.float32)
    y = xf * lax.rsqrt(jnp.mean(xf * xf, axis=-1, keepdims=True) + EPS)
    return (y * g.astype(jnp.float32)).astype(x.dtype)


def rope_tables(seq, dim):
    inv = ROPE_THETA ** (-jnp.arange(0, dim, 2, dtype=jnp.float32) / dim)
    ang = jnp.arange(seq, dtype=jnp.float32)[:, None] * inv[None, :]
    return jnp.cos(ang), jnp.sin(ang)


def apply_rope(x, cos, sin):
    xf = x.astype(jnp.float32)
    half = x.shape[-1] // 2
    x1, x2 = xf[..., :half], xf[..., half:]
    out = jnp.concatenate([x1 * cos - x2 * sin, x2 * cos + x1 * sin], axis=-1)
    return out.astype(x.dtype)


def alibi_slopes(n_heads):
    return jnp.asarray(2.0 ** (-8.0 * np.arange(1, n_heads + 1) / n_heads), dtype=jnp.float32)


def multi_scale_pool(u):
    B, S, _ = u.shape
    uf = u.astype(jnp.float32).reshape(B, S, POOL_GROUPS, POOL_GROUP_DIM)
    cs = jnp.concatenate([jnp.zeros((B, 1, POOL_GROUPS, POOL_GROUP_DIM), jnp.float32),
                          jnp.cumsum(uf, axis=1)], axis=1)
    t = jnp.arange(S)
    outs = []
    for g, w in enumerate(POOL_WINDOWS):
        lo = jnp.clip(t - w // 2, 0, S)
        hi = jnp.clip(t + w // 2, 0, S)
        csg = cs[:, :, g, :]
        win_sum = csg[:, hi] - csg[:, lo]
        cnt = (hi - lo).astype(jnp.float32)[None, :, None]
        outs.append(win_sum / cnt - uf[:, :, g, :])
    return jnp.stack(outs, axis=2).astype(u.dtype)


def mla_attention(q, k, v):
    B, S, H, Dq = q.shape
    Dv = v.shape[-1]
    nb = S // Q_BLOCK
    qb = q.reshape(B, nb, Q_BLOCK, H, Dq).transpose(1, 0, 2, 3, 4)
    scale = Dq ** -0.5

    def block(qblk):
        s = jnp.einsum('bqhd,bkhd->bhqk', qblk, k).astype(jnp.float32) * scale
        p = jax.nn.softmax(s, axis=-1).astype(v.dtype)
        return jnp.einsum('bhqk,bkhd->bqhd', p, v)

    o = lax.map(block, qb)
    return o.transpose(1, 0, 2, 3, 4).reshape(B, S, H * Dv)


def diff_attention(q, k, v, lam, slopes):
    B, S, H, _, d = q.shape
    nb = S // Q_BLOCK
    qb = q.reshape(B, nb, Q_BLOCK, H, 2, d).transpose(1, 0, 2, 3, 4, 5)
    starts = jnp.arange(nb) * Q_BLOCK
    kpos = jnp.arange(S)
    scale = d ** -0.5

    def block(args):
        qblk, start = args
        qpos = start + jnp.arange(Q_BLOCK)
        dist = jnp.abs(qpos[:, None] - kpos[None, :]).astype(jnp.float32)
        bias = -slopes[:, None, None] * dist[None]
        s = jnp.einsum('bqhmd,bkhmd->mbhqk', qblk, k).astype(jnp.float32) * scale + bias[None, None]
        p = jax.nn.softmax(s, axis=-1)
        a = (p[0] - lam * p[1]).astype(v.dtype)
        return jnp.einsum('bhqk,bkhe->bqhe', a, v)

    o = lax.map(block, (qb, starts))
    return o.transpose(1, 0, 2, 3, 4).reshape(B, S, H, 2 * d)


def hybrid_pool_mla(hn, w_in, pool_w, pool_scale, q_norm_g, w_uq, kv_norm_g, w_ukv, w_out, cos, sin):
    B, S, _ = hn.shape
    proj = hn @ w_in
    o1 = POOL_WIDTH
    o2 = o1 + MLA_Q_RANK
    o3 = o2 + MLA_KV_RANK
    u_pool, c_q, c_kv, k_r = proj[..., :o1], proj[..., o1:o2], proj[..., o2:o3], proj[..., o3:]

    pooled = multi_scale_pool(u_pool)
    a_out = jnp.einsum('bsgc,gcd->bsgd', pooled, pool_w).reshape(B, S, POOL_WIDTH) * pool_scale

    q = (rmsnorm(c_q, q_norm_g) @ w_uq).reshape(B, S, MLA_HEADS, MLA_QK_DIM)
    q_nope, q_rope = q[..., :MLA_NOPE_DIM], q[..., MLA_NOPE_DIM:]
    q_rope = apply_rope(q_rope, cos[None, :, None, :], sin[None, :, None, :])
    kv = (rmsnorm(c_kv, kv_norm_g) @ w_ukv).reshape(B, S, MLA_HEADS, MLA_NOPE_DIM + MLA_V_DIM)
    k_nope, v = kv[..., :MLA_NOPE_DIM], kv[..., MLA_NOPE_DIM:]
    k_rope = apply_rope(k_r, cos[None], sin[None])
    k = jnp.concatenate([k_nope, jnp.broadcast_to(k_rope[:, :, None, :], (B, S, MLA_HEADS, MLA_ROPE_DIM))], axis=-1)
    q = jnp.concatenate([q_nope, q_rope], axis=-1)
    b_out = mla_attention(q, k, v)

    return jnp.concatenate([a_out, b_out], axis=-1) @ w_out


def diff_block(hn, w_qkv, lq1, lk1, lq2, lk2, subln_g, w_out, lam_init, slopes):
    B, S, _ = hn.shape
    qkv = hn @ w_qkv
    q = qkv[..., :DIFF_QK_WIDTH].reshape(B, S, DIFF_HEADS, 2, DIFF_HEAD_DIM)
    k = qkv[..., DIFF_QK_WIDTH:2 * DIFF_QK_WIDTH].reshape(B, S, DIFF_HEADS, 2, DIFF_HEAD_DIM)
    v = qkv[..., 2 * DIFF_QK_WIDTH:].reshape(B, S, DIFF_HEADS, DIFF_V_DIM)
    lam = (jnp.exp(jnp.sum(lq1.astype(jnp.float32) * lk1.astype(jnp.float32)))
           - jnp.exp(jnp.sum(lq2.astype(jnp.float32) * lk2.astype(jnp.float32))) + lam_init)
    o = diff_attention(q, k, v, lam, slopes)
    o = rmsnorm(o, subln_g) * (1.0 - lam_init)
    return o.reshape(B, S, DIFF_WIDTH) @ w_out


def swiglu(h, w_gate, w_up, w_down):
    return (jax.nn.silu(h @ w_gate) * (h @ w_up)) @ w_down


def setup_inputs(seed: int = 0) -> dict:
    key = jax.random.key(seed)
    ks = jax.random.split(key, 24)
    f32 = jnp.float32

    def nrm(k, shape, fan_in):
        return jax.random.normal(k, shape, f32) * (fan_in ** -0.5)

    def gain(k, shape):
        return 1.0 + 0.05 * jax.random.normal(k, shape, f32)

    return {
        "x": jax.random.normal(ks[0], (BATCH, SEQ, D_MODEL), f32),
        "attn_norm_g": gain(ks[1], (DEPTH, D_MODEL)),
        "ffn_norm_g": gain(ks[2], (DEPTH, D_MODEL)),
        "final_norm_g": gain(ks[3], (D_MODEL,)),
        "hyb_w_in": nrm(ks[4], (N_EVEN, D_MODEL, HYB_IN), D_MODEL),
        "pool_w": nrm(ks[5], (N_EVEN, POOL_GROUPS, POOL_GROUP_DIM, POOL_GROUP_DIM), POOL_GROUP_DIM),
        "pool_scale": 1.0 + 0.1 * jax.random.normal(ks[6], (N_EVEN, POOL_WIDTH), f32),
        "mla_q_norm_g": gain(ks[7], (N_EVEN, MLA_Q_RANK)),
        "mla_w_uq": nrm(ks[8], (N_EVEN, MLA_Q_RANK, MLA_HEADS * MLA_QK_DIM), MLA_Q_RANK),
        "mla_kv_norm_g": gain(ks[9], (N_EVEN, MLA_KV_RANK)),
        "mla_w_ukv": nrm(ks[10], (N_EVEN, MLA_KV_RANK, MLA_HEADS * (MLA_NOPE_DIM + MLA_V_DIM)), MLA_KV_RANK),
        "hyb_w_out": nrm(ks[11], (N_EVEN, HYB_OUT, D_MODEL), HYB_OUT),
        "diff_w_qkv": nrm(ks[12], (N_ODD, D_MODEL, 2 * DIFF_QK_WIDTH + DIFF_WIDTH), D_MODEL),
        "diff_lambda_q1": 0.1 * jax.random.normal(ks[13], (N_ODD, DIFF_HEAD_DIM), f32),
        "diff_lambda_k1": 0.1 * jax.random.normal(ks[14], (N_ODD, DIFF_HEAD_DIM), f32),
        "diff_lambda_q2": 0.1 * jax.random.normal(ks[15], (N_ODD, DIFF_HEAD_DIM), f32),
        "diff_lambda_k2": 0.1 * jax.random.normal(ks[16], (N_ODD, DIFF_HEAD_DIM), f32),
        "diff_subln_g": gain(ks[17], (N_ODD, DIFF_V_DIM)),
        "diff_w_out": nrm(ks[18], (N_ODD, DIFF_WIDTH, D_MODEL), DIFF_WIDTH),
        "ffn_w_gate": nrm(ks[19], (DEPTH, D_MODEL, D_FF), D_MODEL),
        "ffn_w_up": nrm(ks[20], (DEPTH, D_MODEL, D_FF), D_MODEL),
        "ffn_w_down": nrm(ks[21], (DEPTH, D_FF, D_MODEL), D_FF),
    }


def reference(x, attn_norm_g, ffn_norm_g, final_norm_g, hyb_w_in, pool_w, pool_scale,
              mla_q_norm_g, mla_w_uq, mla_kv_norm_g, mla_w_ukv, hyb_w_out,
              diff_w_qkv, diff_lambda_q1, diff_lambda_k1, diff_lambda_q2, diff_lambda_k2,
              diff_subln_g, diff_w_out, ffn_w_gate, ffn_w_up, ffn_w_down):
    S = x.shape[1]
    cos, sin = rope_tables(S, MLA_ROPE_DIM)
    slopes = alibi_slopes(DIFF_HEADS)
    h = x
    for layer in range(DEPTH):
        hn = rmsnorm(h, attn_norm_g[layer])
        i = layer // 2
        if layer % 2 == 0:
            mix = hybrid_pool_mla(hn, hyb_w_in[i], pool_w[i], pool_scale[i], mla_q_norm_g[i],
                                  mla_w_uq[i], mla_kv_norm_g[i], mla_w_ukv[i], hyb_w_out[i], cos, sin)
        else:
            lam_init = 0.8 - 0.6 * math.exp(-0.3 * layer)
            mix = diff_block(hn, diff_w_qkv[i], diff_lambda_q1[i], diff_lambda_k1[i],
                             diff_lambda_q2[i], diff_lambda_k2[i], diff_subln_g[i], diff_w_out[i],
                             lam_init, slopes)
        h = h + mix
        h = h + swiglu(rmsnorm(h, ffn_norm_g[layer]), ffn_w_gate[layer], ffn_w_up[layer], ffn_w_down[layer])
    return rmsnorm(h, final_norm_g)
```

```python
import functools
import math

import jax
import jax.numpy as jnp
from jax import lax
from jax.experimental import pallas as pl
from jax.experimental.pallas import tpu as pltpu

D_MODEL = 1024
EPS = 1e-6

POOL_WINDOWS = (2, 4, 8, 16)
POOL_GROUP_DIM = 128
POOL_WIDTH = 512
POOL_HALO = 8

MLA_HEADS = 8
MLA_NOPE = 64
MLA_ROPE = 32
MLA_V = 64
MLA_Q_RANK = 256
MLA_KV_RANK = 256
ROPE_THETA = 10000.0
HEAD_LANES = 128
PROJ_WIDTH = POOL_WIDTH + MLA_Q_RANK + MLA_KV_RANK + HEAD_LANES

DIFF_HEADS = 8
DIFF_HEAD_DIM = 64
D_FF = 2816

VMEM_LIMIT = 56 * 1024 * 1024

F32 = jnp.float32
BF16 = jnp.bfloat16


def _rms(x, g):
    return x * lax.rsqrt(jnp.mean(x * x, axis=-1, keepdims=True) + EPS) * g


def _dot(a, b):
    return jnp.dot(a, b, preferred_element_type=F32)


def _dot_nt(a, b):
    return lax.dot_general(a, b, (((1,), (1,)), ((), ())), preferred_element_type=F32)


def _const_spec(shape):
    nd = len(shape)
    return pl.BlockSpec(shape, lambda *_: (0,) * nd, pipeline_mode=pl.Buffered(1))


def _norm_proj_kernel(x_ref, g_ref, w_ref, o_ref):
    y = _rms(x_ref[...], g_ref[...])
    o_ref[...] = _dot(y.astype(BF16), w_ref[...]).astype(o_ref.dtype)


def _norm_proj(x, g, w, out_dtype, tm, name):
    n, d = x.shape
    nout = w.shape[1]
    return pl.pallas_call(
        _norm_proj_kernel,
        grid=(n // tm,),
        in_specs=[pl.BlockSpec((tm, d), lambda i: (i, 0)),
                  _const_spec((1, d)),
                  _const_spec((d, nout))],
        out_specs=pl.BlockSpec((tm, nout), lambda i: (i, 0)),
        out_shape=jax.ShapeDtypeStruct((n, nout), out_dtype),
        compiler_params=pltpu.CompilerParams(
            dimension_semantics=("arbitrary",), vmem_limit_bytes=VMEM_LIMIT),
        name=name,
    )(x, g, w)


def _rope_block(x, c, sa, sb):
    return x * c + pltpu.roll(x, 112, 1) * sa + pltpu.roll(x, 16, 1) * sb


def _mla_prep_kernel(cur_ref, prev_ref, next_ref, pool_w_ref, pool_scale_ref,
                     qg_ref, wuq_ref, kvg_ref, wukv_ref,
                     qc_ref, qsa_ref, qsb_ref, kc_ref, ksa_ref, ksb_ref,
                     a_ref, q_ref, k_ref, v_ref, pad_ref, *, ts, seq):
    i = pl.program_id(1)
    n_tiles = pl.num_programs(1)

    pad_ref[0:POOL_HALO, :] = jnp.where(i > 0, prev_ref[0], 0.0)
    pad_ref[POOL_HALO:POOL_HALO + ts, :] = cur_ref[0, :, 0:POOL_WIDTH]
    pad_ref[POOL_HALO + ts:2 * POOL_HALO + ts, :] = jnp.where(i < n_tiles - 1, next_ref[0], 0.0)
    t = i * ts + lax.broadcasted_iota(jnp.int32, (ts, 1), 0)
    for g, w in enumerate(POOL_WINDOWS):
        lanes = slice(g * POOL_GROUP_DIM, (g + 1) * POOL_GROUP_DIM)
        acc = None
        for off in range(-(w // 2), w // 2):
            piece = pad_ref[POOL_HALO + off:POOL_HALO + off + ts, lanes]
            acc = piece if acc is None else acc + piece
        cnt = jnp.minimum(t + w // 2, seq) - jnp.maximum(t - w // 2, 0)
        inv_cnt = 1.0 / cnt.astype(F32)
        pooled = acc * inv_cnt - cur_ref[0, :, lanes]
        a_g = _dot(pooled.astype(BF16), pool_w_ref[g]) * pool_scale_ref[:, lanes]
        a_ref[0, :, lanes] = a_g.astype(a_ref.dtype)

    c_q = cur_ref[0, :, POOL_WIDTH:POOL_WIDTH + MLA_Q_RANK]
    q = _dot(_rms(c_q, qg_ref[...]).astype(BF16), wuq_ref[...])
    qc, qsa, qsb = qc_ref[...], qsa_ref[...], qsb_ref[...]
    for h in range(MLA_HEADS):
        lanes = slice(h * HEAD_LANES, (h + 1) * HEAD_LANES)
        q_ref[0, :, lanes] = _rope_block(q[:, lanes], qc, qsa, qsb).astype(q_ref.dtype)

    o_kv = POOL_WIDTH + MLA_Q_RANK
    c_kv = cur_ref[0, :, o_kv:o_kv + MLA_KV_RANK]
    kv = _dot(_rms(c_kv, kvg_ref[...]).astype(BF16), wukv_ref[...])
    k_r = cur_ref[0, :, o_kv + MLA_KV_RANK:o_kv + MLA_KV_RANK + HEAD_LANES]
    k_rope = _rope_block(k_r, kc_ref[...], ksa_ref[...], ksb_ref[...])
    for h in range(MLA_HEADS):
        lanes = slice(h * HEAD_LANES, (h + 1) * HEAD_LANES)
        k_ref[0, :, lanes] = (kv[:, lanes] + k_rope).astype(k_ref.dtype)
    v_ref[0] = kv[:, MLA_HEADS * HEAD_LANES:].astype(v_ref.dtype)


def _mla_prep(proj, pool_w, pool_scale, qg, wuq, kvg, wukv, q_tabs, k_tabs, ts):
    b, s, _ = proj.shape
    hb = ts // POOL_HALO
    n_halo = s // POOL_HALO
    tab_spec = pl.BlockSpec((ts, HEAD_LANES), lambda bi, i: (i, 0))
    kern = functools.partial(_mla_prep_kernel, ts=ts, seq=s)
    return pl.pallas_call(
        kern,
        grid=(b, s // ts),
        in_specs=[
            pl.BlockSpec((1, ts, PROJ_WIDTH), lambda bi, i: (bi, i, 0)),
            pl.BlockSpec((1, POOL_HALO, POOL_WIDTH),
                         lambda bi, i: (bi, jnp.maximum(i * hb - 1, 0), 0)),
            pl.BlockSpec((1, POOL_HALO, POOL_WIDTH),
                         lambda bi, i: (bi, jnp.minimum((i + 1) * hb, n_halo - 1), 0)),
            _const_spec(pool_w.shape),
            _const_spec(pool_scale.shape),
            _const_spec(qg.shape),
            _const_spec(wuq.shape),
            _const_spec(kvg.shape),
            _const_spec(wukv.shape),
            tab_spec, tab_spec, tab_spec, tab_spec, tab_spec, tab_spec,
        ],
        out_specs=[
            pl.BlockSpec((1, ts, POOL_WIDTH), lambda bi, i: (bi, i, 0)),
            pl.BlockSpec((1, ts, MLA_HEADS * HEAD_LANES), lambda bi, i: (bi, i, 0)),
            pl.BlockSpec((1, ts, MLA_HEADS * HEAD_LANES), lambda bi, i: (bi, i, 0)),
            pl.BlockSpec((1, ts, MLA_HEADS * MLA_V), lambda bi, i: (bi, i, 0)),
        ],
        out_shape=[
            jax.ShapeDtypeStruct((b, s, POOL_WIDTH), BF16),
            jax.ShapeDtypeStruct((b, s, MLA_HEADS * HEAD_LANES), BF16),
            jax.ShapeDtypeStruct((b, s, MLA_HEADS * HEAD_LANES), BF16),
            jax.ShapeDtypeStruct((b, s, MLA_HEADS * MLA_V), BF16),
        ],
        scratch_shapes=[pltpu.VMEM((ts + 2 * POOL_HALO, POOL_WIDTH), F32)],
        compiler_params=pltpu.CompilerParams(
            dimension_semantics=("arbitrary", "arbitrary"), vmem_limit_bytes=VMEM_LIMIT),
        name="mla_prep",
    )(proj, proj, proj, pool_w, pool_scale, qg, wuq, kvg, wukv, *q_tabs, *k_tabs)


def _mla_attn_kernel(q_ref, k_ref, v_ref, o_ref):
    v = v_ref[0]
    outs = []
    for j in range(2):
        lanes = slice(j * HEAD_LANES, (j + 1) * HEAD_LANES)
        s = _dot_nt(q_ref[0, :, lanes], k_ref[0, :, lanes])
        e = jnp.exp(s - jnp.max(s, axis=-1, keepdims=True))
        l = jnp.sum(e, axis=-1, keepdims=True)
        outs.append(_dot(e.astype(BF16), v) * (1.0 / l))
    lane = lax.broadcasted_iota(jnp.int32, outs[0].shape, 1)
    o_ref[0] = jnp.where(lane < MLA_V, outs[0], outs[1]).astype(o_ref.dtype)


def _mla_attn(q, k, v, tq):
    b, s, _ = q.shape
    pairs = MLA_HEADS // 2
    return pl.pallas_call(
        _mla_attn_kernel,
        grid=(b, pairs, s // tq),
        in_specs=[
            pl.BlockSpec((1, tq, 2 * HEAD_LANES), lambda bi, p, i: (bi, i, p)),
            pl.BlockSpec((1, s, 2 * HEAD_LANES), lambda bi, p, i: (bi, 0, p)),
            pl.BlockSpec((1, s, 2 * MLA_V), lambda bi, p, i: (bi, 0, p)),
        ],
        out_specs=pl.BlockSpec((1, tq, 2 * MLA_V), lambda bi, p, i: (bi, i, p)),
        out_shape=jax.ShapeDtypeStruct((b, s, MLA_HEADS * MLA_V), BF16),
        compiler_params=pltpu.CompilerParams(
            dimension_semantics=("arbitrary",) * 3, vmem_limit_bytes=VMEM_LIMIT),
        name="mla_attn",
    )(q, k, v)


def _diff_attn_kernel(q_ref, k_ref, v_ref, slope_ref, lq1_ref, lk1_ref, lq2_ref, lk2_ref,
                      g_ref, o_ref, bias_ref, *, tq, lam_init):
    qi = pl.program_id(1)
    bi = pl.program_id(2)

    @pl.when(bi == 0)
    def _():
        qpos = qi * tq + lax.broadcasted_iota(jnp.int32, bias_ref.shape, 0)
        kpos = lax.broadcasted_iota(jnp.int32, bias_ref.shape, 1)
        bias_ref[...] = jnp.abs(qpos - kpos).astype(F32) * slope_ref[0, :, 0:1]

    lam = (jnp.exp(jnp.sum(lq1_ref[...] * lk1_ref[...], axis=-1, keepdims=True))
           - jnp.exp(jnp.sum(lq2_ref[...] * lk2_ref[...], axis=-1, keepdims=True))
           + lam_init)

    q = q_ref[0]
    k = k_ref[0]
    lane = lax.broadcasted_iota(jnp.int32, q.shape, 1)
    zero = jnp.zeros_like(q)
    bias = bias_ref[...]
    probs = []
    for m in range(2):
        q_m = jnp.where((lane >= m * DIFF_HEAD_DIM) & (lane < (m + 1) * DIFF_HEAD_DIM), q, zero)
        s = _dot_nt(q_m, k) + bias
        e = jnp.exp(s - jnp.max(s, axis=-1, keepdims=True))
        probs.append((e, 1.0 / jnp.sum(e, axis=-1, keepdims=True)))
    a = probs[0][0] * probs[0][1] - probs[1][0] * (lam * probs[1][1])
    o = _dot(a.astype(BF16), v_ref[0])
    o = _rms(o, g_ref[...]) * (1.0 - lam_init)
    o_ref[0] = o.astype(o_ref.dtype)


def _diff_attn(qkv, neg_slopes, lq1, lk1, lq2, lk2, subln_g, lam_init, tq):
    b, s, _ = qkv.shape
    h = DIFF_HEADS
    kern = functools.partial(_diff_attn_kernel, tq=tq, lam_init=lam_init)
    vec = lambda a: _const_spec(a.shape)
    return pl.pallas_call(
        kern,
        grid=(h, s // tq, b),
        in_specs=[
            pl.BlockSpec((1, tq, HEAD_LANES), lambda hi, i, bi: (bi, i, hi)),
            pl.BlockSpec((1, s, HEAD_LANES), lambda hi, i, bi: (bi, 0, h + hi)),
            pl.BlockSpec((1, s, HEAD_LANES), lambda hi, i, bi: (bi, 0, 2 * h + hi)),
            pl.BlockSpec((1, 1, HEAD_LANES), lambda hi, i, bi: (hi, 0, 0)),
            vec(lq1), vec(lk1), vec(lq2), vec(lk2), vec(subln_g),
        ],
        out_specs=pl.BlockSpec((1, tq, HEAD_LANES), lambda hi, i, bi: (bi, i, hi)),
        out_shape=jax.ShapeDtypeStruct((b, s, h * HEAD_LANES), BF16),
        scratch_shapes=[pltpu.VMEM((tq, s), F32)],
        compiler_params=pltpu.CompilerParams(
            dimension_semantics=("arbitrary",) * 3, vmem_limit_bytes=VMEM_LIMIT),
        name="diff_attn",
    )(qkv, qkv, qkv, neg_slopes, lq1, lk1, lq2, lk2, subln_g)


def _post_kernel(*refs, n_mix, final_norm):
    h_ref = refs[0]
    mix_refs = refs[1:1 + n_mix]
    wo_refs = refs[1 + n_mix:1 + 2 * n_mix]
    ng_ref, wg_ref, wu_ref, wd_ref = refs[1 + 2 * n_mix:5 + 2 * n_mix]
    rest = refs[5 + 2 * n_mix:]
    fg_ref = rest[0] if final_norm else None
    o_ref = rest[-1]

    h = h_ref[...]
    for m_ref, w_ref in zip(mix_refs, wo_refs):
        h = h + _dot(m_ref[...], w_ref[...])
    hn = _rms(h, ng_ref[...]).astype(BF16)
    gate = _dot(hn, wg_ref[...])
    up = _dot(hn, wu_ref[...])
    act = gate * jax.nn.sigmoid(gate) * up
    h = h + _dot(act.astype(BF16), wd_ref[...])
    if final_norm:
        h = _rms(h, fg_ref[...])
    o_ref[...] = h


def _post(h, mixes, w_outs, norm_g, w_gate, w_up, w_down, final_g, tm):
    n, d = h.shape
    n_mix = len(mixes)
    final_norm = final_g is not None
    kern = functools.partial(_post_kernel, n_mix=n_mix, final_norm=final_norm)
    row_spec = lambda a: pl.BlockSpec((tm, a.shape[1]), lambda i: (i, 0))
    consts = list(w_outs) + [norm_g, w_gate, w_up, w_down] + ([final_g] if final_norm else [])
    return pl.pallas_call(
        kern,
        grid=(n // tm,),
        in_specs=[row_spec(h)] + [row_spec(m) for m in mixes]
                 + [_const_spec(c.shape) for c in consts],
        out_specs=pl.BlockSpec((tm, d), lambda i: (i, 0)),
        out_shape=jax.ShapeDtypeStruct((n, d), F32),
        compiler_params=pltpu.CompilerParams(
            dimension_semantics=("arbitrary",), vmem_limit_bytes=VMEM_LIMIT),
        name="post_final" if final_norm else "post",
    )(h, *mixes, *consts)


def _pad_cols(w, width):
    return jnp.pad(w, ((0, 0), (0, width - w.shape[1])))


def _hybrid_weights(w_in, w_uq, w_ukv):
    o1 = POOL_WIDTH + MLA_Q_RANK + MLA_KV_RANK
    k_r = jnp.pad(w_in[:, o1:], ((0, 0), (MLA_NOPE, HEAD_LANES - MLA_NOPE - MLA_ROPE)))
    w_in_p = jnp.concatenate([w_in[:, :o1], k_r], axis=1).astype(BF16)
    r = w_uq.shape[0]
    wq = w_uq.reshape(r, MLA_HEADS, MLA_NOPE + MLA_ROPE)
    wq = jnp.pad(wq, ((0, 0), (0, 0), (0, HEAD_LANES - MLA_NOPE - MLA_ROPE)))
    wq = wq.reshape(r, MLA_HEADS * HEAD_LANES).astype(BF16)
    wkv = w_ukv.reshape(r, MLA_HEADS, MLA_NOPE + MLA_V)
    wk = jnp.pad(wkv[:, :, :MLA_NOPE], ((0, 0), (0, 0), (0, HEAD_LANES - MLA_NOPE)))
    wk = wk.reshape(r, MLA_HEADS * HEAD_LANES)
    wv = wkv[:, :, MLA_NOPE:].reshape(r, MLA_HEADS * MLA_V)
    wkv_p = jnp.concatenate([wk, wv], axis=1).astype(BF16)
    return w_in_p, wq, wkv_p


def _rope_tables(seq, scale):
    half = MLA_ROPE // 2
    inv = ROPE_THETA ** (-jnp.arange(0, MLA_ROPE, 2, dtype=F32) / MLA_ROPE)
    ang = jnp.arange(seq, dtype=F32)[:, None] * inv[None, :]
    cos, sin = jnp.cos(ang), jnp.sin(ang)
    z = lambda w: jnp.zeros((seq, w), F32)
    tail = HEAD_LANES - MLA_NOPE - MLA_ROPE
    c = jnp.concatenate([jnp.ones((seq, MLA_NOPE), F32), cos, cos, z(tail)], axis=1)
    sa = jnp.concatenate([z(MLA_NOPE), -sin, z(half), z(tail)], axis=1)
    sb = jnp.concatenate([z(MLA_NOPE), z(half), sin, z(tail)], axis=1)
    return c * scale, sa * scale, sb * scale


def kernel(x, attn_norm_g, ffn_norm_g, final_norm_g, hyb_w_in, pool_w, pool_scale,
           mla_q_norm_g, mla_w_uq, mla_kv_norm_g, mla_w_ukv, hyb_w_out,
           diff_w_qkv, diff_lambda_q1, diff_lambda_k1, diff_lambda_q2, diff_lambda_k2,
           diff_subln_g, diff_w_out, ffn_w_gate, ffn_w_up, ffn_w_down):
    b, s, d = x.shape
    n = b * s
    row = lambda a: a.reshape(1, -1)
    h = x.reshape(n, d)

    w_in_p, wq_p, wkv_p = _hybrid_weights(hyb_w_in[0], mla_w_uq[0], mla_w_ukv[0])
    proj = _norm_proj(h, row(attn_norm_g[0]), w_in_p, F32, 512, "norm_proj_in")
    q_tabs = _rope_tables(s, float((MLA_NOPE + MLA_ROPE) ** -0.5))
    k_tabs = _rope_tables(s, 1.0)
    a_out, q, k, v = _mla_prep(
        proj.reshape(b, s, PROJ_WIDTH), pool_w[0].astype(BF16), row(pool_scale[0]),
        row(mla_q_norm_g[0]), wq_p, row(mla_kv_norm_g[0]), wkv_p, q_tabs, k_tabs, 512)
    b_out = _mla_attn(q, k, v, 256)
    w_out0 = hyb_w_out[0].astype(BF16)
    h = _post(h, [a_out.reshape(n, -1), b_out.reshape(n, -1)],
              [w_out0[:POOL_WIDTH], w_out0[POOL_WIDTH:]],
              row(ffn_norm_g[0]), ffn_w_gate[0].astype(BF16), ffn_w_up[0].astype(BF16),
              ffn_w_down[0].astype(BF16), None, 512)

    qk_w = DIFF_HEADS * 2 * DIFF_HEAD_DIM
    w_qkv = diff_w_qkv[0]
    w_qkv = jnp.concatenate([w_qkv[:, :qk_w] * (DIFF_HEAD_DIM ** -0.5), w_qkv[:, qk_w:]], axis=1)
    qkv = _norm_proj(h, row(attn_norm_g[1]), w_qkv.astype(BF16), BF16, 512, "norm_proj_qkv")
    slopes = 2.0 ** (-8.0 * jnp.arange(1, DIFF_HEADS + 1, dtype=F32) / DIFF_HEADS)
    neg_slopes = jnp.broadcast_to(-slopes[:, None, None], (DIFF_HEADS, 1, HEAD_LANES))
    lam_init = 0.8 - 0.6 * math.exp(-0.3 * 1)
    o = _diff_attn(qkv.reshape(b, s, -1), neg_slopes, row(diff_lambda_q1[0]),
                   row(diff_lambda_k1[0]), row(diff_lambda_q2[0]), row(diff_lambda_k2[0]),
                   row(diff_subln_g[0]), lam_init, 256)
    h = _post(h, [o.reshape(n, -1)], [diff_w_out[0].astype(BF16)],
              row(ffn_norm_g[1]), ffn_w_gate[1].astype(BF16), ffn_w_up[1].astype(BF16),
              ffn_w_down[1].astype(BF16), row(final_norm_g), 512)
    return h.reshape(b, s, d)
```

```python
import functools
import math

import jax
import jax.numpy as jnp
from jax import lax
from jax.experimental import pallas as pl
from jax.experimental.pallas import tpu as pltpu

D_MODEL = 1024
EPS = 1e-6
LOG2E = math.log2(math.e)

POOL_WINDOWS = (2, 4, 8, 16)
POOL_GROUP_DIM = 128
POOL_WIDTH = 512
POOL_HALO = 8

MLA_HEADS = 8
MLA_NOPE = 64
MLA_ROPE = 32
MLA_V = 64
MLA_Q_RANK = 256
MLA_KV_RANK = 256
ROPE_THETA = 10000.0
HEAD_LANES = 128
PROJ_WIDTH = POOL_WIDTH + MLA_Q_RANK + MLA_KV_RANK + HEAD_LANES

DIFF_HEADS = 8
DIFF_HEAD_DIM = 64
D_FF = 2816

SUBLANES = 8
BF16_ROWS = 16
KEY_CHUNK = 512
ATTN_TQ = 512
MLA_GROUP = 4
DIFF_GROUP = 2

VMEM_LIMIT = 56 * 1024 * 1024

F32 = jnp.float32
BF16 = jnp.bfloat16


def _rms(x, g):
    return x * lax.rsqrt(jnp.mean(x * x, axis=-1, keepdims=True) + EPS) * g


def _dot(a, b):
    return jnp.dot(a, b, preferred_element_type=F32)


def _dot_nt(a, b):
    return lax.dot_general(a, b, (((1,), (1,)), ((), ())), preferred_element_type=F32)


def _const_spec(shape):
    nd = len(shape)
    return pl.BlockSpec(shape, lambda *_: (0,) * nd, pipeline_mode=pl.Buffered(1))


def _params(n_axes):
    return pltpu.CompilerParams(dimension_semantics=("arbitrary",) * n_axes,
                                vmem_limit_bytes=VMEM_LIMIT)


def _norm_proj_kernel(x_ref, g_ref, w_ref, o_ref):
    y = _rms(x_ref[...], g_ref[...])
    o_ref[...] = _dot(y.astype(BF16), w_ref[...]).astype(o_ref.dtype)


def _norm_proj(x, g, w, out_dtype, tm, name):
    n, d = x.shape
    nout = w.shape[1]
    return pl.pallas_call(
        _norm_proj_kernel,
        grid=(n // tm,),
        in_specs=[pl.BlockSpec((tm, d), lambda i: (i, 0)),
                  _const_spec((1, d)),
                  _const_spec((d, nout))],
        out_specs=pl.BlockSpec((tm, nout), lambda i: (i, 0)),
        out_shape=jax.ShapeDtypeStruct((n, nout), out_dtype),
        compiler_params=_params(1),
        name=name,
    )(x, g, w)


def _norm_qkv_kernel(x_ref, g_ref, wqk_ref, wvt_ref, qk_ref, vt_ref):
    y = _rms(x_ref[0], g_ref[...]).astype(BF16)
    qk_ref[0] = _dot(y, wqk_ref[...]).astype(qk_ref.dtype)
    vt_ref[0] = _dot_nt(wvt_ref[...], y).astype(vt_ref.dtype)


def _norm_qkv(x, g, w_qk, w_vt, tm):
    b, s, d = x.shape
    return pl.pallas_call(
        _norm_qkv_kernel,
        grid=(b, s // tm),
        in_specs=[pl.BlockSpec((1, tm, d), lambda bi, i: (bi, i, 0)),
                  _const_spec(g.shape), _const_spec(w_qk.shape), _const_spec(w_vt.shape)],
        out_specs=[pl.BlockSpec((1, tm, w_qk.shape[1]), lambda bi, i: (bi, i, 0)),
                   pl.BlockSpec((1, w_vt.shape[0], tm), lambda bi, i: (bi, 0, i))],
        out_shape=[jax.ShapeDtypeStruct((b, s, w_qk.shape[1]), BF16),
                   jax.ShapeDtypeStruct((b, w_vt.shape[0], s), BF16)],
        compiler_params=_params(2),
        name="norm_qkv",
    )(x, g, w_qk, w_vt)


def _rope_block(x, c, sa, sb):
    return x * c + pltpu.roll(x, 112, 1) * sa + pltpu.roll(x, 16, 1) * sb


def _mla_prep_kernel(cur_ref, prev_ref, next_ref, pool_w_ref, pool_scale_ref,
                     qg_ref, wuq_ref, kvg_ref, wuk_ref, wuvt_ref,
                     qc_ref, qsa_ref, qsb_ref, kc_ref, ksa_ref, ksb_ref,
                     a_ref, q_ref, k_ref, vt_ref, pad_ref, *, ts, seq):
    i = pl.program_id(1)
    n_tiles = pl.num_programs(1)

    pad_ref[0:POOL_HALO, :] = jnp.where(i > 0, prev_ref[0], 0.0)
    pad_ref[POOL_HALO:POOL_HALO + ts, :] = cur_ref[0, :, 0:POOL_WIDTH]
    pad_ref[POOL_HALO + ts:2 * POOL_HALO + ts, :] = jnp.where(i < n_tiles - 1, next_ref[0], 0.0)
    t = i * ts + lax.broadcasted_iota(jnp.int32, (ts, 1), 0)
    for g, w in enumerate(POOL_WINDOWS):
        lanes = slice(g * POOL_GROUP_DIM, (g + 1) * POOL_GROUP_DIM)
        acc = None
        for off in range(-(w // 2), w // 2):
            piece = pad_ref[POOL_HALO + off:POOL_HALO + off + ts, lanes]
            acc = piece if acc is None else acc + piece
        cnt = jnp.minimum(t + w // 2, seq) - jnp.maximum(t - w // 2, 0)
        inv_cnt = 1.0 / cnt.astype(F32)
        pooled = acc * inv_cnt - cur_ref[0, :, lanes]
        a_g = _dot(pooled.astype(BF16), pool_w_ref[g]) * pool_scale_ref[:, lanes]
        a_ref[0, :, lanes] = a_g.astype(a_ref.dtype)

    c_q = cur_ref[0, :, POOL_WIDTH:POOL_WIDTH + MLA_Q_RANK]
    q = _dot(_rms(c_q, qg_ref[...]).astype(BF16), wuq_ref[...])
    qc, qsa, qsb = qc_ref[...], qsa_ref[...], qsb_ref[...]
    for h in range(MLA_HEADS):
        lanes = slice(h * HEAD_LANES, (h + 1) * HEAD_LANES)
        q_ref[0, :, lanes] = _rope_block(q[:, lanes], qc, qsa, qsb).astype(q_ref.dtype)

    o_kv = POOL_WIDTH + MLA_Q_RANK
    c_kv = cur_ref[0, :, o_kv:o_kv + MLA_KV_RANK]
    kvn = _rms(c_kv, kvg_ref[...]).astype(BF16)
    k_nope = _dot(kvn, wuk_ref[...])
    k_r = cur_ref[0, :, o_kv + MLA_KV_RANK:o_kv + MLA_KV_RANK + HEAD_LANES]
    k_rope = _rope_block(k_r, kc_ref[...], ksa_ref[...], ksb_ref[...])
    for h in range(MLA_HEADS):
        lanes = slice(h * HEAD_LANES, (h + 1) * HEAD_LANES)
        k_ref[0, :, lanes] = (k_nope[:, lanes] + k_rope).astype(k_ref.dtype)
    vt_ref[0] = _dot_nt(wuvt_ref[...], kvn).astype(vt_ref.dtype)


def _mla_prep(proj, pool_w, pool_scale, qg, wuq, kvg, wuk, wuvt, q_tabs, k_tabs, ts):
    b, s, _ = proj.shape
    hb = ts // POOL_HALO
    n_halo = s // POOL_HALO
    tab_spec = pl.BlockSpec((ts, HEAD_LANES), lambda bi, i: (i, 0))
    kern = functools.partial(_mla_prep_kernel, ts=ts, seq=s)
    consts = [pool_w, pool_scale, qg, wuq, kvg, wuk, wuvt]
    row_blk = lambda w: pl.BlockSpec((1, ts, w), lambda bi, i: (bi, i, 0))
    return pl.pallas_call(
        kern,
        grid=(b, s // ts),
        in_specs=[
            row_blk(PROJ_WIDTH),
            pl.BlockSpec((1, POOL_HALO, POOL_WIDTH),
                         lambda bi, i: (bi, jnp.maximum(i * hb - 1, 0), 0)),
            pl.BlockSpec((1, POOL_HALO, POOL_WIDTH),
                         lambda bi, i: (bi, jnp.minimum((i + 1) * hb, n_halo - 1), 0)),
        ] + [_const_spec(c.shape) for c in consts] + [tab_spec] * 6,
        out_specs=[
            row_blk(POOL_WIDTH),
            row_blk(MLA_HEADS * HEAD_LANES),
            row_blk(MLA_HEADS * HEAD_LANES),
            pl.BlockSpec((1, MLA_HEADS * MLA_V, ts), lambda bi, i: (bi, 0, i)),
        ],
        out_shape=[
            jax.ShapeDtypeStruct((b, s, POOL_WIDTH), BF16),
            jax.ShapeDtypeStruct((b, s, MLA_HEADS * HEAD_LANES), BF16),
            jax.ShapeDtypeStruct((b, s, MLA_HEADS * HEAD_LANES), BF16),
            jax.ShapeDtypeStruct((b, MLA_HEADS * MLA_V, s), BF16),
        ],
        scratch_shapes=[pltpu.VMEM((ts + 2 * POOL_HALO, POOL_WIDTH), F32)],
        compiler_params=_params(2),
        name="mla_prep",
    )(proj, proj, proj, *consts, *q_tabs, *k_tabs)


def _ones_rows(cols):
    row = lax.broadcasted_iota(jnp.int32, (BF16_ROWS, cols), 0)
    return jnp.where(row == 0, 1.0, 0.0).astype(BF16)


def _attend(n_items, n_keys, scores, values, s_ref):
    n_chunks = n_keys // KEY_CHUNK
    tq = s_ref.shape[-1]

    def pass1(i, c, m8):
        s = scores(i, c)
        s_ref[i % 2, c * KEY_CHUNK:(c + 1) * KEY_CHUNK, :] = s
        m_c = jnp.max(s.reshape(KEY_CHUNK // SUBLANES, SUBLANES, tq), axis=0)
        return m_c if m8 is None else jnp.maximum(m8, m_c)

    m8 = None
    for c in range(n_chunks):
        m8 = pass1(0, c, m8)
    outs = []
    for i in range(n_items):
        m = jnp.max(m8, axis=0, keepdims=True)
        m8 = None
        acc = None
        for c in range(n_chunks):
            if i + 1 < n_items:
                m8 = pass1(i + 1, c, m8)
            e = jnp.exp2(s_ref[i % 2, c * KEY_CHUNK:(c + 1) * KEY_CHUNK, :] - m)
            pv = _dot(values(i, c), e.astype(BF16))
            acc = pv if acc is None else acc + pv
        outs.append(acc)
    return outs


def _mla_attn_kernel(q_ref, k_ref, vt_ref, o_ref, s_ref):
    ones = _ones_rows(KEY_CHUNK)

    def scores(j, c):
        lanes = slice(j * HEAD_LANES, (j + 1) * HEAD_LANES)
        keys = slice(c * KEY_CHUNK, (c + 1) * KEY_CHUNK)
        return _dot_nt(k_ref[0, keys, lanes], q_ref[0, :, lanes])

    def values(j, c):
        vt = vt_ref[0, j * MLA_V:(j + 1) * MLA_V, c * KEY_CHUNK:(c + 1) * KEY_CHUNK]
        return jnp.concatenate([vt, ones], axis=0)

    accs = _attend(MLA_GROUP, k_ref.shape[1], scores, values, s_ref)
    outs = [a[:MLA_V] * (1.0 / a[MLA_V:MLA_V + 1]) for a in accs]
    o_ref[0] = jnp.concatenate(outs, axis=0).T.astype(o_ref.dtype)


def _mla_attn(q, k, vt):
    b, s, _ = q.shape
    tq, g = ATTN_TQ, MLA_GROUP
    return pl.pallas_call(
        _mla_attn_kernel,
        grid=(b, MLA_HEADS // g, s // tq),
        in_specs=[
            pl.BlockSpec((1, tq, g * HEAD_LANES), lambda bi, p, i: (bi, i, p)),
            pl.BlockSpec((1, s, g * HEAD_LANES), lambda bi, p, i: (bi, 0, p)),
            pl.BlockSpec((1, g * MLA_V, s), lambda bi, p, i: (bi, p, 0)),
        ],
        out_specs=pl.BlockSpec((1, tq, g * MLA_V), lambda bi, p, i: (bi, i, p)),
        out_shape=jax.ShapeDtypeStruct((b, s, MLA_HEADS * MLA_V), BF16),
        scratch_shapes=[pltpu.VMEM((2, s, tq), F32)],
        compiler_params=_params(3),
        name="mla_attn",
    )(q, k, vt)


def _diff_attn_kernel(q_ref, k_ref, vt_ref, slope_ref, lq1_ref, lk1_ref, lq2_ref, lk2_ref,
                      g_ref, o_ref, bias_ref, s_ref, *, lam_init):
    qi = pl.program_id(1)
    bi = pl.program_id(2)
    tq = s_ref.shape[-1]

    @pl.when(bi == 0)
    def _():
        kpos = lax.broadcasted_iota(jnp.int32, bias_ref.shape[1:], 0)
        qpos = qi * tq + lax.broadcasted_iota(jnp.int32, bias_ref.shape[1:], 1)
        dist = jnp.abs(qpos - kpos).astype(F32)
        for g in range(DIFF_GROUP):
            bias_ref[g] = dist * slope_ref[g, :, 0:1]

    lam = (jnp.exp(jnp.sum(lq1_ref[...] * lk1_ref[...], axis=-1, keepdims=True))
           - jnp.exp(jnp.sum(lq2_ref[...] * lk2_ref[...], axis=-1, keepdims=True))
           + lam_init)

    lane = lax.broadcasted_iota(jnp.int32, (tq, HEAD_LANES), 1)
    q_maps = []
    for g in range(DIFF_GROUP):
        q = q_ref[0, :, g * HEAD_LANES:(g + 1) * HEAD_LANES]
        for m in range(2):
            in_map = (lane >= m * DIFF_HEAD_DIM) & (lane < (m + 1) * DIFF_HEAD_DIM)
            q_maps.append(jnp.where(in_map, q, jnp.zeros_like(q)))
    ones = _ones_rows(KEY_CHUNK)

    def scores(i, c):
        g = i // 2
        keys = slice(c * KEY_CHUNK, (c + 1) * KEY_CHUNK)
        k = k_ref[0, keys, g * HEAD_LANES:(g + 1) * HEAD_LANES]
        return _dot_nt(k, q_maps[i]) + bias_ref[g, keys, :]

    def values(i, c):
        g = i // 2
        vt = vt_ref[0, g * HEAD_LANES:(g + 1) * HEAD_LANES, c * KEY_CHUNK:(c + 1) * KEY_CHUNK]
        return jnp.concatenate([vt, ones], axis=0)

    accs = _attend(2 * DIFF_GROUP, k_ref.shape[1], scores, values, s_ref)
    d = HEAD_LANES
    for g in range(DIFF_GROUP):
        a0, a1 = accs[2 * g], accs[2 * g + 1]
        o_t = a0[:d] * (1.0 / a0[d:d + 1]) - a1[:d] * (lam / a1[d:d + 1])
        o = _rms(o_t.T, g_ref[...]) * (1.0 - lam_init)
        o_ref[0, :, g * d:(g + 1) * d] = o.astype(o_ref.dtype)


def _diff_attn(qk, vt, neg_slopes, lq1, lk1, lq2, lk2, subln_g, lam_init):
    b, s, _ = qk.shape
    tq, g = ATTN_TQ, DIFF_GROUP
    n_groups = DIFF_HEADS // g
    kern = functools.partial(_diff_attn_kernel, lam_init=lam_init)
    vec = lambda a: _const_spec(a.shape)
    return pl.pallas_call(
        kern,
        grid=(n_groups, s // tq, b),
        in_specs=[
            pl.BlockSpec((1, tq, g * HEAD_LANES), lambda hi, i, bi: (bi, i, hi)),
            pl.BlockSpec((1, s, g * HEAD_LANES), lambda hi, i, bi: (bi, 0, n_groups + hi)),
            pl.BlockSpec((1, g * HEAD_LANES, s), lambda hi, i, bi: (bi, hi, 0)),
            pl.BlockSpec((g, 1, HEAD_LANES), lambda hi, i, bi: (hi, 0, 0)),
            vec(lq1), vec(lk1), vec(lq2), vec(lk2), vec(subln_g),
        ],
        out_specs=pl.BlockSpec((1, tq, g * HEAD_LANES), lambda hi, i, bi: (bi, i, hi)),
        out_shape=jax.ShapeDtypeStruct((b, s, DIFF_HEADS * HEAD_LANES), BF16),
        scratch_shapes=[pltpu.VMEM((g, s, tq), F32), pltpu.VMEM((2, s, tq), F32)],
        compiler_params=_params(3),
        name="diff_attn",
    )(qk, qk, vt, neg_slopes, lq1, lk1, lq2, lk2, subln_g)


def _post_kernel(*refs, n_mix, final_norm):
    h_ref = refs[0]
    mix_refs = refs[1:1 + n_mix]
    wo_refs = refs[1 + n_mix:1 + 2 * n_mix]
    ng_ref, wg_ref, wu_ref, wd_ref = refs[1 + 2 * n_mix:5 + 2 * n_mix]
    rest = refs[5 + 2 * n_mix:]
    fg_ref = rest[0] if final_norm else None
    o_ref = rest[-1]

    h = h_ref[...]
    for m_ref, w_ref in zip(mix_refs, wo_refs):
        h = h + _dot(m_ref[...], w_ref[...])
    hn = _rms(h, ng_ref[...]).astype(BF16)
    gate = _dot(hn, wg_ref[...])
    up = _dot(hn, wu_ref[...])
    act = gate * jax.nn.sigmoid(gate) * up
    h = h + _dot(act.astype(BF16), wd_ref[...])
    if final_norm:
        h = _rms(h, fg_ref[...])
    o_ref[...] = h


def _post(h, mixes, w_outs, norm_g, w_gate, w_up, w_down, final_g, tm):
    n, d = h.shape
    n_mix = len(mixes)
    final_norm = final_g is not None
    kern = functools.partial(_post_kernel, n_mix=n_mix, final_norm=final_norm)
    row_spec = lambda a: pl.BlockSpec((tm, a.shape[1]), lambda i: (i, 0))
    consts = list(w_outs) + [norm_g, w_gate, w_up, w_down] + ([final_g] if final_norm else [])
    return pl.pallas_call(
        kern,
        grid=(n // tm,),
        in_specs=[row_spec(h)] + [row_spec(m) for m in mixes]
                 + [_const_spec(c.shape) for c in consts],
        out_specs=pl.BlockSpec((tm, d), lambda i: (i, 0)),
        out_shape=jax.ShapeDtypeStruct((n, d), F32),
        compiler_params=_params(1),
        name="post_final" if final_norm else "post",
    )(h, *mixes, *consts)


def _hybrid_weights(w_in, w_uq, w_ukv):
    o1 = POOL_WIDTH + MLA_Q_RANK + MLA_KV_RANK
    k_r = jnp.pad(w_in[:, o1:], ((0, 0), (MLA_NOPE, HEAD_LANES - MLA_NOPE - MLA_ROPE)))
    w_in_p = jnp.concatenate([w_in[:, :o1], k_r], axis=1).astype(BF16)
    r = w_uq.shape[0]
    wq = w_uq.reshape(r, MLA_HEADS, MLA_NOPE + MLA_ROPE)
    wq = jnp.pad(wq, ((0, 0), (0, 0), (0, HEAD_LANES - MLA_NOPE - MLA_ROPE)))
    wq = wq.reshape(r, MLA_HEADS * HEAD_LANES).astype(BF16)
    wkv = w_ukv.reshape(r, MLA_HEADS, MLA_NOPE + MLA_V)
    wk = jnp.pad(wkv[:, :, :MLA_NOPE], ((0, 0), (0, 0), (0, HEAD_LANES - MLA_NOPE)))
    wk = wk.reshape(r, MLA_HEADS * HEAD_LANES).astype(BF16)
    wvt = wkv[:, :, MLA_NOPE:].reshape(r, MLA_HEADS * MLA_V).T.astype(BF16)
    return w_in_p, wq, wk, wvt


def _rope_tables(seq, scale):
    half = MLA_ROPE // 2
    inv = ROPE_THETA ** (-jnp.arange(0, MLA_ROPE, 2, dtype=F32) / MLA_ROPE)
    ang = jnp.arange(seq, dtype=F32)[:, None] * inv[None, :]
    cos, sin = jnp.cos(ang), jnp.sin(ang)
    z = lambda w: jnp.zeros((seq, w), F32)
    tail = HEAD_LANES - MLA_NOPE - MLA_ROPE
    c = jnp.concatenate([jnp.ones((seq, MLA_NOPE), F32), cos, cos, z(tail)], axis=1)
    sa = jnp.concatenate([z(MLA_NOPE), -sin, z(half), z(tail)], axis=1)
    sb = jnp.concatenate([z(MLA_NOPE), z(half), sin, z(tail)], axis=1)
    return c * scale, sa * scale, sb * scale


def kernel(x, attn_norm_g, ffn_norm_g, final_norm_g, hyb_w_in, pool_w, pool_scale,
           mla_q_norm_g, mla_w_uq, mla_kv_norm_g, mla_w_ukv, hyb_w_out,
           diff_w_qkv, diff_lambda_q1, diff_lambda_k1, diff_lambda_q2, diff_lambda_k2,
           diff_subln_g, diff_w_out, ffn_w_gate, ffn_w_up, ffn_w_down):
    b, s, d = x.shape
    n = b * s
    row = lambda a: a.reshape(1, -1)
    h = x.reshape(n, d)

    w_in_p, wq_p, wk_p, wvt_p = _hybrid_weights(hyb_w_in[0], mla_w_uq[0], mla_w_ukv[0])
    proj = _norm_proj(h, row(attn_norm_g[0]), w_in_p, F32, 512, "norm_proj_in")
    q_tabs = _rope_tables(s, float((MLA_NOPE + MLA_ROPE) ** -0.5 * LOG2E))
    k_tabs = _rope_tables(s, 1.0)
    a_out, q, k, vt = _mla_prep(
        proj.reshape(b, s, PROJ_WIDTH), pool_w[0].astype(BF16), row(pool_scale[0]),
        row(mla_q_norm_g[0]), wq_p, row(mla_kv_norm_g[0]), wk_p, wvt_p, q_tabs, k_tabs, 512)
    b_out = _mla_attn(q, k, vt)
    w_out0 = hyb_w_out[0].astype(BF16)
    h = _post(h, [a_out.reshape(n, -1), b_out.reshape(n, -1)],
              [w_out0[:POOL_WIDTH], w_out0[POOL_WIDTH:]],
              row(ffn_norm_g[0]), ffn_w_gate[0].astype(BF16), ffn_w_up[0].astype(BF16),
              ffn_w_down[0].astype(BF16), None, 512)

    qk_w = DIFF_HEADS * 2 * DIFF_HEAD_DIM
    w_qkv = diff_w_qkv[0]
    w_qk = jnp.concatenate([w_qkv[:, :qk_w] * (DIFF_HEAD_DIM ** -0.5 * LOG2E),
                            w_qkv[:, qk_w:2 * qk_w]], axis=1).astype(BF16)
    w_vt = w_qkv[:, 2 * qk_w:].T.astype(BF16)
    qk, vt = _norm_qkv(h.reshape(b, s, d), row(attn_norm_g[1]), w_qk, w_vt, 512)
    slopes = 2.0 ** (-8.0 * jnp.arange(1, DIFF_HEADS + 1, dtype=F32) / DIFF_HEADS)
    neg_slopes = jnp.broadcast_to((-LOG2E * slopes)[:, None, None], (DIFF_HEADS, 1, HEAD_LANES))
    lam_init = 0.8 - 0.6 * math.exp(-0.3 * 1)
    o = _diff_attn(qk, vt, neg_slopes, row(diff_lambda_q1[0]),
                   row(diff_lambda_k1[0]), row(diff_lambda_q2[0]), row(diff_lambda_k2[0]),
                   row(diff_subln_g[0]), lam_init)
    h = _post(h, [o.reshape(n, -1)], [diff_w_out[0].astype(BF16)],
              row(ffn_norm_g[1]), ffn_w_gate[1].astype(BF16), ffn_w_up[1].astype(BF16),
              ffn_w_down[1].astype(BF16), row(final_norm_g), 512)
    return h.reshape(b, s, d)
```

```python
import functools
import math

import jax
import jax.numpy as jnp
from jax import lax
from jax.experimental import pallas as pl
from jax.experimental.pallas import tpu as pltpu

D_MODEL = 1024
EPS = 1e-6
LOG2E = math.log2(math.e)

POOL_WINDOWS = (2, 4, 8, 16)
POOL_GROUP_DIM = 128
POOL_WIDTH = 512
POOL_HALO = 8

MLA_HEADS = 8
MLA_NOPE = 64
MLA_ROPE = 32
MLA_V = 64
MLA_Q_RANK = 256
MLA_KV_RANK = 256
ROPE_THETA = 10000.0
HEAD_LANES = 128
PROJ_WIDTH = POOL_WIDTH + MLA_Q_RANK + MLA_KV_RANK + HEAD_LANES

DIFF_HEADS = 8
DIFF_HEAD_DIM = 64
D_FF = 2816

SUBLANES = 8
BF16_ROWS = 16
KEY_CHUNK = 512
ATTN_TQ = 512
MLA_GROUP = 4
DIFF_GROUP = 4

VMEM_LIMIT = 56 * 1024 * 1024

F32 = jnp.float32
BF16 = jnp.bfloat16


def _rms(x, g):
    return x * lax.rsqrt(jnp.mean(x * x, axis=-1, keepdims=True) + EPS) * g


def _dot(a, b):
    return jnp.dot(a, b, preferred_element_type=F32)


def _dot_nt(a, b):
    return lax.dot_general(a, b, (((1,), (1,)), ((), ())), preferred_element_type=F32)


def _const_spec(shape):
    nd = len(shape)
    return pl.BlockSpec(shape, lambda *_: (0,) * nd, pipeline_mode=pl.Buffered(1))


def _params(n_axes):
    return pltpu.CompilerParams(dimension_semantics=("arbitrary",) * n_axes,
                                vmem_limit_bytes=VMEM_LIMIT)


def _norm_proj_kernel(x_ref, g_ref, w_ref, o_ref):
    y = _rms(x_ref[...], g_ref[...])
    o_ref[...] = _dot(y.astype(BF16), w_ref[...]).astype(o_ref.dtype)


def _norm_proj(x, g, w, out_dtype, tm, name):
    n, d = x.shape
    nout = w.shape[1]
    return pl.pallas_call(
        _norm_proj_kernel,
        grid=(n // tm,),
        in_specs=[pl.BlockSpec((tm, d), lambda i: (i, 0)),
                  _const_spec((1, d)),
                  _const_spec((d, nout))],
        out_specs=pl.BlockSpec((tm, nout), lambda i: (i, 0)),
        out_shape=jax.ShapeDtypeStruct((n, nout), out_dtype),
        compiler_params=_params(1),
        name=name,
    )(x, g, w)


def _norm_qkv_kernel(x_ref, g_ref, wqk_ref, wvt_ref, qk_ref, vt_ref):
    y = _rms(x_ref[0], g_ref[...]).astype(BF16)
    qk_ref[0] = _dot(y, wqk_ref[...]).astype(qk_ref.dtype)
    vt_ref[0, 0] = _dot_nt(wvt_ref[...], y).astype(vt_ref.dtype)


def _norm_qkv(x, g, w_qk, w_vt, tm):
    b, s, d = x.shape
    return pl.pallas_call(
        _norm_qkv_kernel,
        grid=(b, s // tm),
        in_specs=[pl.BlockSpec((1, tm, d), lambda bi, i: (bi, i, 0)),
                  _const_spec(g.shape), _const_spec(w_qk.shape), _const_spec(w_vt.shape)],
        out_specs=[pl.BlockSpec((1, tm, w_qk.shape[1]), lambda bi, i: (bi, i, 0)),
                   pl.BlockSpec((1, 1, w_vt.shape[0], tm), lambda bi, i: (bi, i, 0, 0))],
        out_shape=[jax.ShapeDtypeStruct((b, s, w_qk.shape[1]), BF16),
                   jax.ShapeDtypeStruct((b, s // tm, w_vt.shape[0], tm), BF16)],
        compiler_params=_params(2),
        name="norm_qkv",
    )(x, g, w_qk, w_vt)


def _rope_block(x, c, sa, sb):
    return x * c + pltpu.roll(x, 112, 1) * sa + pltpu.roll(x, 16, 1) * sb


def _mla_prep_kernel(cur_ref, prev_ref, next_ref, pool_w_ref, pool_scale_ref,
                     qg_ref, wuq_ref, kvg_ref, wuk_ref, wuvt_ref,
                     qc_ref, qsa_ref, qsb_ref, kc_ref, ksa_ref, ksb_ref,
                     a_ref, q_ref, k_ref, vt_ref, pad_ref, *, ts, seq):
    i = pl.program_id(1)
    n_tiles = pl.num_programs(1)

    pad_ref[0:POOL_HALO, :] = jnp.where(i > 0, prev_ref[0], 0.0)
    pad_ref[POOL_HALO:POOL_HALO + ts, :] = cur_ref[0, :, 0:POOL_WIDTH]
    pad_ref[POOL_HALO + ts:2 * POOL_HALO + ts, :] = jnp.where(i < n_tiles - 1, next_ref[0], 0.0)
    t = i * ts + lax.broadcasted_iota(jnp.int32, (ts, 1), 0)
    for g, w in enumerate(POOL_WINDOWS):
        lanes = slice(g * POOL_GROUP_DIM, (g + 1) * POOL_GROUP_DIM)
        acc = None
        for off in range(-(w // 2), w // 2):
            piece = pad_ref[POOL_HALO + off:POOL_HALO + off + ts, lanes]
            acc = piece if acc is None else acc + piece
        cnt = jnp.minimum(t + w // 2, seq) - jnp.maximum(t - w // 2, 0)
        inv_cnt = 1.0 / cnt.astype(F32)
        pooled = acc * inv_cnt - cur_ref[0, :, lanes]
        a_g = _dot(pooled.astype(BF16), pool_w_ref[g]) * pool_scale_ref[:, lanes]
        a_ref[0, :, lanes] = a_g.astype(a_ref.dtype)

    c_q = cur_ref[0, :, POOL_WIDTH:POOL_WIDTH + MLA_Q_RANK]
    q = _dot(_rms(c_q, qg_ref[...]).astype(BF16), wuq_ref[...])
    qc, qsa, qsb = qc_ref[...], qsa_ref[...], qsb_ref[...]
    for h in range(MLA_HEADS):
        lanes = slice(h * HEAD_LANES, (h + 1) * HEAD_LANES)
        q_ref[0, :, lanes] = _rope_block(q[:, lanes], qc, qsa, qsb).astype(q_ref.dtype)

    o_kv = POOL_WIDTH + MLA_Q_RANK
    c_kv = cur_ref[0, :, o_kv:o_kv + MLA_KV_RANK]
    kvn = _rms(c_kv, kvg_ref[...]).astype(BF16)
    k_nope = _dot(kvn, wuk_ref[...])
    k_r = cur_ref[0, :, o_kv + MLA_KV_RANK:o_kv + MLA_KV_RANK + HEAD_LANES]
    k_rope = _rope_block(k_r, kc_ref[...], ksa_ref[...], ksb_ref[...])
    for h in range(MLA_HEADS):
        lanes = slice(h * HEAD_LANES, (h + 1) * HEAD_LANES)
        k_ref[0, :, lanes] = (k_nope[:, lanes] + k_rope).astype(k_ref.dtype)
    vt_ref[0] = _dot_nt(wuvt_ref[...], kvn).astype(vt_ref.dtype)


def _mla_prep(proj, pool_w, pool_scale, qg, wuq, kvg, wuk, wuvt, q_tabs, k_tabs, ts):
    b, s, _ = proj.shape
    hb = ts // POOL_HALO
    n_halo = s // POOL_HALO
    tab_spec = pl.BlockSpec((ts, HEAD_LANES), lambda bi, i: (i, 0))
    kern = functools.partial(_mla_prep_kernel, ts=ts, seq=s)
    consts = [pool_w, pool_scale, qg, wuq, kvg, wuk, wuvt]
    row_blk = lambda w: pl.BlockSpec((1, ts, w), lambda bi, i: (bi, i, 0))
    return pl.pallas_call(
        kern,
        grid=(b, s // ts),
        in_specs=[
            row_blk(PROJ_WIDTH),
            pl.BlockSpec((1, POOL_HALO, POOL_WIDTH),
                         lambda bi, i: (bi, jnp.maximum(i * hb - 1, 0), 0)),
            pl.BlockSpec((1, POOL_HALO, POOL_WIDTH),
                         lambda bi, i: (bi, jnp.minimum((i + 1) * hb, n_halo - 1), 0)),
        ] + [_const_spec(c.shape) for c in consts] + [tab_spec] * 6,
        out_specs=[
            row_blk(POOL_WIDTH),
            row_blk(MLA_HEADS * HEAD_LANES),
            row_blk(MLA_HEADS * HEAD_LANES),
            pl.BlockSpec((1, MLA_HEADS * MLA_V, ts), lambda bi, i: (bi, 0, i)),
        ],
        out_shape=[
            jax.ShapeDtypeStruct((b, s, POOL_WIDTH), BF16),
            jax.ShapeDtypeStruct((b, s, MLA_HEADS * HEAD_LANES), BF16),
            jax.ShapeDtypeStruct((b, s, MLA_HEADS * HEAD_LANES), BF16),
            jax.ShapeDtypeStruct((b, MLA_HEADS * MLA_V, s), BF16),
        ],
        scratch_shapes=[pltpu.VMEM((ts + 2 * POOL_HALO, POOL_WIDTH), F32)],
        compiler_params=_params(2),
        name="mla_prep",
    )(proj, proj, proj, *consts, *q_tabs, *k_tabs)


def _ones_rows(cols):
    row = lax.broadcasted_iota(jnp.int32, (BF16_ROWS, cols), 0)
    return jnp.where(row == 0, 1.0, 0.0).astype(BF16)


SWEEP_ROWS = 16


def _attend_tiles(n_tiles, n_items, n_keys, scores, values, finish, s_ref, e_ref, extra=None):
    n_chunks = n_keys // KEY_CHUNK
    tq = s_ref.shape[-1]
    assert n_items % 2 == 0 and n_items >= 2 and n_chunks % 2 == 0

    def park(t, i, c):
        s_ref[i % 2, c * KEY_CHUNK:(c + 1) * KEY_CHUNK, :] = scores(t, i, c)

    def logits(i, c, r):
        blk = s_ref[i % 2, c * KEY_CHUNK + r:c * KEY_CHUNK + r + SWEEP_ROWS, :]
        add = None if extra is None else extra(i % n_items, c, r)
        return blk if add is None else blk + add

    def block_max(i, c, r, m8):
        blk = logits(i, c, r).reshape(SWEEP_ROWS // SUBLANES, SUBLANES, tq)
        for part in range(SWEEP_ROWS // SUBLANES):
            m8 = blk[part] if m8 is None else jnp.maximum(m8, blk[part])
        return m8

    m8 = None
    for c in range(n_chunks):
        park(0, 0, c)
    for c in range(n_chunks):
        for r in range(0, KEY_CHUNK, SWEEP_ROWS):
            m8 = block_max(0, c, r, m8)
    park(0, 1, 0)

    def tile(t, m8):
        t_next = jnp.minimum(t + 1, n_tiles - 1)
        accs = [None] * n_items

        def value_matmul(pos):
            i, c = divmod(pos, n_chunks)
            pv = _dot(values(t, i, c), e_ref[pos % 2])
            accs[i] = pv if accs[i] is None else accs[i] + pv

        for i in range(n_items):
            m = jnp.broadcast_to(jnp.max(m8, axis=0, keepdims=True), (SWEEP_ROWS, tq))
            m8 = None
            for c in range(n_chunks):
                pos = i * n_chunks + c
                ahead_i, ahead_c = divmod(pos + n_chunks + 1, n_chunks)
                if ahead_i < n_items:
                    park(t, ahead_i, ahead_c)
                else:
                    park(t_next, ahead_i - n_items, ahead_c)
                for r in range(0, KEY_CHUNK, SWEEP_ROWS):
                    e_ref[pos % 2, r:r + SWEEP_ROWS, :] = (
                        jnp.exp2(logits(i, c, r) - m).astype(BF16))
                    m8 = block_max(i + 1, c, r, m8)
                if pos > 0:
                    value_matmul(pos - 1)
        value_matmul(n_items * n_chunks - 1)
        finish(t, accs)
        return m8

    lax.fori_loop(0, n_tiles, tile, m8)


def _mla_attn_kernel(q_ref, k_ref, vt_ref, o_ref, s_ref, e_ref):
    tq = s_ref.shape[-1]
    ones = _ones_rows(KEY_CHUNK)

    def tile_rows(t):
        return pl.ds(pl.multiple_of(t * tq, tq), tq)

    q_t = {}

    def scores(t, j, c):
        lanes = slice(j * HEAD_LANES, (j + 1) * HEAD_LANES)
        keys = slice(c * KEY_CHUNK, (c + 1) * KEY_CHUNK)
        if (id(t), j) not in q_t:
            q_t[id(t), j] = q_ref[0, tile_rows(t), lanes].T
        return _dot(k_ref[0, keys, lanes], q_t[id(t), j])

    def values(t, j, c):
        vt = vt_ref[0, j * MLA_V:(j + 1) * MLA_V, c * KEY_CHUNK:(c + 1) * KEY_CHUNK]
        return jnp.concatenate([vt, ones], axis=0)

    def finish(t, accs):
        outs = [a[:MLA_V] * (1.0 / a[MLA_V:MLA_V + 1]) for a in accs]
        o_ref[0, tile_rows(t), :] = jnp.concatenate(outs, axis=0).T.astype(o_ref.dtype)

    _attend_tiles(q_ref.shape[1] // tq, MLA_GROUP, k_ref.shape[1], scores, values, finish,
                  s_ref, e_ref)


def _mla_attn(q, k, vt):
    b, s, _ = q.shape
    tq, g = ATTN_TQ, MLA_GROUP
    return pl.pallas_call(
        _mla_attn_kernel,
        grid=(b, MLA_HEADS // g),
        in_specs=[
            pl.BlockSpec((1, s, g * HEAD_LANES), lambda bi, p: (bi, 0, p)),
            pl.BlockSpec((1, s, g * HEAD_LANES), lambda bi, p: (bi, 0, p)),
            pl.BlockSpec((1, g * MLA_V, s), lambda bi, p: (bi, p, 0)),
        ],
        out_specs=pl.BlockSpec((1, s, g * MLA_V), lambda bi, p: (bi, 0, p)),
        out_shape=jax.ShapeDtypeStruct((b, s, MLA_HEADS * MLA_V), BF16),
        scratch_shapes=[pltpu.VMEM((2, s, tq), F32), pltpu.VMEM((2, KEY_CHUNK, tq), BF16)],
        compiler_params=_params(2),
        name="mla_attn",
    )(q, k, vt)


def _bf16_pieces(x):
    p1 = x.astype(BF16).astype(F32)
    p2 = (x - p1).astype(BF16).astype(F32)
    p3 = (x - p1 - p2).astype(BF16).astype(F32)
    return p1, p2, p3


def _piece_lanes(lane, pieces):
    sel = lane % 3
    return jnp.where(sel == 0, pieces[0], jnp.where(sel == 1, pieces[1], pieces[2]))


def _diff_attn_kernel(q_ref, k_ref, vt_ref, slope_ref, lq1_ref, lk1_ref, lq2_ref, lk2_ref,
                      g_ref, o_ref, kaug_ref, dbias_ref, s_ref, e_ref, *, lam_init):
    bi = pl.program_id(1)
    tq = s_ref.shape[-1]
    n_keys = k_ref.shape[1]
    n_chunks = n_keys // KEY_CHUNK
    n_tiles = q_ref.shape[1] // tq
    half = KEY_CHUNK // 2
    aug = HEAD_LANES

    slopes = [slope_ref[g, :, 0:1] for g in range(DIFF_GROUP)]
    pieces = [_bf16_pieces(c) for c in slopes]

    @pl.when(bi == 0)
    def _():
        j = lax.broadcasted_iota(jnp.int32, (n_keys, aug), 0)
        lane = lax.broadcasted_iota(jnp.int32, (n_keys, aug), 1)
        chunk = j // KEY_CHUNK
        dj = j % KEY_CHUNK
        lo = (dj % half).astype(F32)
        dk = lax.broadcasted_iota(jnp.int32, (KEY_CHUNK, tq), 0)
        dq = lax.broadcasted_iota(jnp.int32, (KEY_CHUNK, tq), 1)
        dist = jnp.abs(dq - dk).astype(F32)
        for g in range(DIFF_GROUP):
            neg_c = -_piece_lanes(lane, pieces[g])
            dbias_ref[g] = -slopes[g] * dist
            for qi in range(n_tiles):
                side = jnp.sign(qi - chunk).astype(F32)
                hi = (dj - dj % half - (qi - chunk) * KEY_CHUNK).astype(F32)
                val = jnp.where(lane < 6, neg_c,
                                jnp.where(lane < 9, lo, jnp.where(lane < 12, hi, 0.0)))
                kaug_ref[qi, g, :, aug:] = (side * val).astype(BF16)

    lam = (jnp.exp(jnp.sum(lq1_ref[...] * lk1_ref[...], axis=-1, keepdims=True))
           - jnp.exp(jnp.sum(lq2_ref[...] * lk2_ref[...], axis=-1, keepdims=True))
           + lam_init)

    lane = lax.broadcasted_iota(jnp.int32, (tq, HEAD_LANES), 1)
    di = lax.broadcasted_iota(jnp.int32, (tq, aug), 0)
    q_lo = (di % half).astype(F32)
    q_hi = (di - di % half).astype(F32)
    q_augs = []
    for g in range(DIFF_GROUP):
        k = k_ref[0, :, g * HEAD_LANES:(g + 1) * HEAD_LANES]
        for qi in range(n_tiles):
            kaug_ref[qi, g, :, :aug] = k
        c_lanes = _piece_lanes(lane, pieces[g])
        q_augs.append(jnp.where(lane < 3, q_lo, jnp.where(lane < 6, q_hi,
                                                           jnp.where(lane < 12, c_lanes, 0.0))))
    ones = _ones_rows(KEY_CHUNK)
    q_maps = {}

    def tile_rows(t):
        return pl.ds(pl.multiple_of(t * tq, tq), tq)

    def chunk_of(t, c):
        return (t + c) % n_chunks

    def scores(t, i, c):
        g, m = divmod(i, 2)
        if (id(t), i) not in q_maps:
            q = q_ref[0, tile_rows(t), g * HEAD_LANES:(g + 1) * HEAD_LANES]
            in_map = (lane >= m * DIFF_HEAD_DIM) & (lane < (m + 1) * DIFF_HEAD_DIM)
            q_m = jnp.where(in_map, q, jnp.zeros_like(q))
            q_cat = jnp.concatenate([q_m.astype(F32), q_augs[g]], axis=1)
            q_maps[id(t), i] = q_cat.T.astype(BF16)
        rows = pl.ds(pl.multiple_of(chunk_of(t, c) * KEY_CHUNK, KEY_CHUNK), KEY_CHUNK)
        return _dot(kaug_ref[t, g, rows, :], q_maps[id(t), i])

    def values(t, i, c):
        g = i // 2
        vt = vt_ref[0, chunk_of(t, c), g * HEAD_LANES:(g + 1) * HEAD_LANES, :]
        return jnp.concatenate([vt, ones], axis=0)

    def diagonal_bias(i, c, r):
        return dbias_ref[i // 2, r:r + SWEEP_ROWS, :] if c == 0 else None

    def finish(t, accs):
        d = HEAD_LANES
        for g in range(DIFF_GROUP):
            a0, a1 = accs[2 * g], accs[2 * g + 1]
            o_t = a0[:d] * (1.0 / a0[d:d + 1]) - a1[:d] * (lam / a1[d:d + 1])
            o = _rms(o_t.T, g_ref[...]) * (1.0 - lam_init)
            o_ref[0, tile_rows(t), g * d:(g + 1) * d] = o.astype(o_ref.dtype)

    _attend_tiles(n_tiles, 2 * DIFF_GROUP, n_keys, scores, values, finish, s_ref, e_ref,
                  diagonal_bias)


def _diff_attn(qk, vt, slopes, lq1, lk1, lq2, lk2, subln_g, lam_init):
    b, s, _ = qk.shape
    tq, g = ATTN_TQ, DIFF_GROUP
    n_groups = DIFF_HEADS // g
    n_chunks = s // KEY_CHUNK
    assert tq == KEY_CHUNK and vt.shape == (b, n_chunks, DIFF_HEADS * HEAD_LANES, KEY_CHUNK)
    kern = functools.partial(_diff_attn_kernel, lam_init=lam_init)
    vec = lambda a: _const_spec(a.shape)
    return pl.pallas_call(
        kern,
        grid=(n_groups, b),
        in_specs=[
            pl.BlockSpec((1, s, g * HEAD_LANES), lambda hi, bi: (bi, 0, hi)),
            pl.BlockSpec((1, s, g * HEAD_LANES), lambda hi, bi: (bi, 0, n_groups + hi)),
            pl.BlockSpec((1, n_chunks, g * HEAD_LANES, KEY_CHUNK), lambda hi, bi: (bi, 0, hi, 0)),
            pl.BlockSpec((g, 1, HEAD_LANES), lambda hi, bi: (hi, 0, 0)),
            vec(lq1), vec(lk1), vec(lq2), vec(lk2), vec(subln_g),
        ],
        out_specs=pl.BlockSpec((1, s, g * HEAD_LANES), lambda hi, bi: (bi, 0, hi)),
        out_shape=jax.ShapeDtypeStruct((b, s, DIFF_HEADS * HEAD_LANES), BF16),
        scratch_shapes=[pltpu.VMEM((s // tq, g, s, 2 * HEAD_LANES), BF16),
                        pltpu.VMEM((g, KEY_CHUNK, tq), F32),
                        pltpu.VMEM((2, s, tq), F32),
                        pltpu.VMEM((2, KEY_CHUNK, tq), BF16)],
        compiler_params=_params(2),
        name="diff_attn",
    )(qk, qk, vt, slopes, lq1, lk1, lq2, lk2, subln_g)


def _post_kernel(*refs, n_mix, final_norm):
    h_ref = refs[0]
    mix_refs = refs[1:1 + n_mix]
    wo_refs = refs[1 + n_mix:1 + 2 * n_mix]
    ng_ref, wg_ref, wu_ref, wd_ref = refs[1 + 2 * n_mix:5 + 2 * n_mix]
    rest = refs[5 + 2 * n_mix:]
    fg_ref = rest[0] if final_norm else None
    o_ref = rest[-1]

    h = h_ref[...]
    for m_ref, w_ref in zip(mix_refs, wo_refs):
        h = h + _dot(m_ref[...], w_ref[...])
    hn = _rms(h, ng_ref[...]).astype(BF16)
    gate = _dot(hn, wg_ref[...])
    up = _dot(hn, wu_ref[...])
    act = gate * jax.nn.sigmoid(gate) * up
    h = h + _dot(act.astype(BF16), wd_ref[...])
    if final_norm:
        h = _rms(h, fg_ref[...])
    o_ref[...] = h


def _post(h, mixes, w_outs, norm_g, w_gate, w_up, w_down, final_g, tm):
    n, d = h.shape
    n_mix = len(mixes)
    final_norm = final_g is not None
    kern = functools.partial(_post_kernel, n_mix=n_mix, final_norm=final_norm)
    row_spec = lambda a: pl.BlockSpec((tm, a.shape[1]), lambda i: (i, 0))
    consts = list(w_outs) + [norm_g, w_gate, w_up, w_down] + ([final_g] if final_norm else [])
    return pl.pallas_call(
        kern,
        grid=(n // tm,),
        in_specs=[row_spec(h)] + [row_spec(m) for m in mixes]
                 + [_const_spec(c.shape) for c in consts],
        out_specs=pl.BlockSpec((tm, d), lambda i: (i, 0)),
        out_shape=jax.ShapeDtypeStruct((n, d), F32),
        compiler_params=_params(1),
        name="post_final" if final_norm else "post",
    )(h, *mixes, *consts)


def _hybrid_weights(w_in, w_uq, w_ukv):
    o1 = POOL_WIDTH + MLA_Q_RANK + MLA_KV_RANK
    k_r = jnp.pad(w_in[:, o1:], ((0, 0), (MLA_NOPE, HEAD_LANES - MLA_NOPE - MLA_ROPE)))
    w_in_p = jnp.concatenate([w_in[:, :o1], k_r], axis=1).astype(BF16)
    r = w_uq.shape[0]
    wq = w_uq.reshape(r, MLA_HEADS, MLA_NOPE + MLA_ROPE)
    wq = jnp.pad(wq, ((0, 0), (0, 0), (0, HEAD_LANES - MLA_NOPE - MLA_ROPE)))
    wq = wq.reshape(r, MLA_HEADS * HEAD_LANES).astype(BF16)
    wkv = w_ukv.reshape(r, MLA_HEADS, MLA_NOPE + MLA_V)
    wk = jnp.pad(wkv[:, :, :MLA_NOPE], ((0, 0), (0, 0), (0, HEAD_LANES - MLA_NOPE)))
    wk = wk.reshape(r, MLA_HEADS * HEAD_LANES).astype(BF16)
    wvt = wkv[:, :, MLA_NOPE:].reshape(r, MLA_HEADS * MLA_V).T.astype(BF16)
    return w_in_p, wq, wk, wvt


def _rope_tables(seq, scale):
    half = MLA_ROPE // 2
    inv = ROPE_THETA ** (-jnp.arange(0, MLA_ROPE, 2, dtype=F32) / MLA_ROPE)
    ang = jnp.arange(seq, dtype=F32)[:, None] * inv[None, :]
    cos, sin = jnp.cos(ang), jnp.sin(ang)
    z = lambda w: jnp.zeros((seq, w), F32)
    tail = HEAD_LANES - MLA_NOPE - MLA_ROPE
    c = jnp.concatenate([jnp.ones((seq, MLA_NOPE), F32), cos, cos, z(tail)], axis=1)
    sa = jnp.concatenate([z(MLA_NOPE), -sin, z(half), z(tail)], axis=1)
    sb = jnp.concatenate([z(MLA_NOPE), z(half), sin, z(tail)], axis=1)
    return c * scale, sa * scale, sb * scale


def kernel(x, attn_norm_g, ffn_norm_g, final_norm_g, hyb_w_in, pool_w, pool_scale,
           mla_q_norm_g, mla_w_uq, mla_kv_norm_g, mla_w_ukv, hyb_w_out,
           diff_w_qkv, diff_lambda_q1, diff_lambda_k1, diff_lambda_q2, diff_lambda_k2,
           diff_subln_g, diff_w_out, ffn_w_gate, ffn_w_up, ffn_w_down):
    b, s, d = x.shape
    n = b * s
    row = lambda a: a.reshape(1, -1)
    h = x.reshape(n, d)

    w_in_p, wq_p, wk_p, wvt_p = _hybrid_weights(hyb_w_in[0], mla_w_uq[0], mla_w_ukv[0])
    proj = _norm_proj(h, row(attn_norm_g[0]), w_in_p, F32, 512, "norm_proj_in")
    q_tabs = _rope_tables(s, float((MLA_NOPE + MLA_ROPE) ** -0.5 * LOG2E))
    k_tabs = _rope_tables(s, 1.0)
    a_out, q, k, vt = _mla_prep(
        proj.reshape(b, s, PROJ_WIDTH), pool_w[0].astype(BF16), row(pool_scale[0]),
        row(mla_q_norm_g[0]), wq_p, row(mla_kv_norm_g[0]), wk_p, wvt_p, q_tabs, k_tabs, 512)
    b_out = _mla_attn(q, k, vt)
    w_out0 = hyb_w_out[0].astype(BF16)
    h = _post(h, [a_out.reshape(n, -1), b_out.reshape(n, -1)],
              [w_out0[:POOL_WIDTH], w_out0[POOL_WIDTH:]],
              row(ffn_norm_g[0]), ffn_w_gate[0].astype(BF16), ffn_w_up[0].astype(BF16),
              ffn_w_down[0].astype(BF16), None, 512)

    qk_w = DIFF_HEADS * 2 * DIFF_HEAD_DIM
    w_qkv = diff_w_qkv[0]
    w_qk = jnp.concatenate([w_qkv[:, :qk_w] * (DIFF_HEAD_DIM ** -0.5 * LOG2E),
                            w_qkv[:, qk_w:2 * qk_w]], axis=1).astype(BF16)
    w_vt = w_qkv[:, 2 * qk_w:].T.astype(BF16)
    qk, vt = _norm_qkv(h.reshape(b, s, d), row(attn_norm_g[1]), w_qk, w_vt, KEY_CHUNK)
    slopes = 2.0 ** (-8.0 * jnp.arange(1, DIFF_HEADS + 1, dtype=F32) / DIFF_HEADS)
    slopes2 = jnp.broadcast_to((LOG2E * slopes)[:, None, None], (DIFF_HEADS, 1, HEAD_LANES))
    lam_init = 0.8 - 0.6 * math.exp(-0.3 * 1)
    o = _diff_attn(qk, vt, slopes2, row(diff_lambda_q1[0]),
                   row(diff_lambda_k1[0]), row(diff_lambda_q2[0]), row(diff_lambda_k2[0]),
                   row(diff_subln_g[0]), lam_init)
    h = _post(h, [o.reshape(n, -1)], [diff_w_out[0].astype(BF16)],
              row(ffn_norm_g[1]), ffn_w_gate[1].astype(BF16), ffn_w_up[1].astype(BF16),
              ffn_w_down[1].astype(BF16), row(final_norm_g), 512)
    return h.reshape(b, s, d)
```

```python
import functools
import math

import jax
import jax.numpy as jnp
from jax import lax
from jax.experimental import pallas as pl
from jax.experimental.pallas import tpu as pltpu

D_MODEL = 1024
EPS = 1e-6
LOG2E = math.log2(math.e)

POOL_WINDOWS = (2, 4, 8, 16)
POOL_GROUP_DIM = 128
POOL_WIDTH = 512
POOL_HALO = 8

MLA_HEADS = 8
MLA_NOPE = 64
MLA_ROPE = 32
MLA_V = 64
MLA_Q_RANK = 256
MLA_KV_RANK = 256
ROPE_THETA = 10000.0
HEAD_LANES = 128
PROJ_WIDTH = POOL_WIDTH + MLA_Q_RANK + MLA_KV_RANK + 2 * HEAD_LANES

DIFF_HEADS = 8
DIFF_HEAD_DIM = 64
D_FF = 2816

SUBLANES = 8
BF16_ROWS = 16
KEY_CHUNK = 512
ATTN_TQ = 512
MLA_GROUP = 4
DIFF_GROUP = 4

VMEM_LIMIT = 56 * 1024 * 1024

F32 = jnp.float32
BF16 = jnp.bfloat16


def _rms(x, g):
    return x * lax.rsqrt(jnp.mean(x * x, axis=-1, keepdims=True) + EPS) * g


def _dot(a, b):
    return jnp.dot(a, b, preferred_element_type=F32)


def _dot_nt(a, b):
    return lax.dot_general(a, b, (((1,), (1,)), ((), ())), preferred_element_type=F32)


def _const_spec(shape):
    nd = len(shape)
    return pl.BlockSpec(shape, lambda *_: (0,) * nd, pipeline_mode=pl.Buffered(1))


def _params(n_axes):
    return pltpu.CompilerParams(dimension_semantics=("arbitrary",) * n_axes,
                                vmem_limit_bytes=VMEM_LIMIT)


def _norm_qkv_kernel(x_ref, g_ref, wqk_ref, wvt_ref, qk_ref, vt_ref):
    y = _rms(x_ref[0], g_ref[...]).astype(BF16)
    qk_ref[0] = _dot(y, wqk_ref[...]).astype(qk_ref.dtype)
    vt_ref[0, 0] = _dot_nt(wvt_ref[...], y).astype(vt_ref.dtype)


def _norm_qkv(x, g, w_qk, w_vt, tm):
    b, s, d = x.shape
    return pl.pallas_call(
        _norm_qkv_kernel,
        grid=(b, s // tm),
        in_specs=[pl.BlockSpec((1, tm, d), lambda bi, i: (bi, i, 0)),
                  _const_spec(g.shape), _const_spec(w_qk.shape), _const_spec(w_vt.shape)],
        out_specs=[pl.BlockSpec((1, tm, w_qk.shape[1]), lambda bi, i: (bi, i, 0)),
                   pl.BlockSpec((1, 1, w_vt.shape[0], tm), lambda bi, i: (bi, i, 0, 0))],
        out_shape=[jax.ShapeDtypeStruct((b, s, w_qk.shape[1]), BF16),
                   jax.ShapeDtypeStruct((b, s // tm, w_vt.shape[0], tm), BF16)],
        compiler_params=_params(2),
        name="norm_qkv",
    )(x, g, w_qk, w_vt)


def _mla_prep_kernel(x_ref, xprev_ref, xnext_ref, ng_ref, win_ref, pool_w_ref, pool_scale_ref,
                     qg_ref, wuq_ref, kvg_ref, wuk_ref, wuvt_ref,
                     qc_ref, qs_ref, kc_ref, ks_ref,
                     a_ref, q_ref, k_ref, vt_ref, cur_ref, pad_ref, *, ts, seq):
    i = pl.program_id(1)
    n_tiles = pl.num_programs(1)

    cur_ref[0] = _dot(_rms(x_ref[0], ng_ref[...]).astype(BF16), win_ref[...])
    x_halo = jnp.concatenate([xprev_ref[0], xnext_ref[0]], axis=0)
    u_halo = _dot(_rms(x_halo, ng_ref[...]).astype(BF16), win_ref[:, 0:POOL_WIDTH])

    pad_ref[0:POOL_HALO, :] = jnp.where(i > 0, u_halo[:POOL_HALO], 0.0)
    pad_ref[POOL_HALO:POOL_HALO + ts, :] = cur_ref[0, :, 0:POOL_WIDTH]
    pad_ref[POOL_HALO + ts:2 * POOL_HALO + ts, :] = jnp.where(i < n_tiles - 1,
                                                              u_halo[POOL_HALO:], 0.0)
    t = i * ts + lax.broadcasted_iota(jnp.int32, (ts, 1), 0)
    for g, w in enumerate(POOL_WINDOWS):
        lanes = slice(g * POOL_GROUP_DIM, (g + 1) * POOL_GROUP_DIM)
        acc = None
        for off in range(-(w // 2), w // 2):
            piece = pad_ref[POOL_HALO + off:POOL_HALO + off + ts, lanes]
            acc = piece if acc is None else acc + piece
        cnt = jnp.minimum(t + w // 2, seq) - jnp.maximum(t - w // 2, 0)
        inv_cnt = 1.0 / cnt.astype(F32)
        pooled = acc * inv_cnt - cur_ref[0, :, lanes]
        a_g = _dot(pooled.astype(BF16), pool_w_ref[g]) * pool_scale_ref[:, lanes]
        a_ref[0, :, lanes] = a_g.astype(a_ref.dtype)

    c_q = cur_ref[0, :, POOL_WIDTH:POOL_WIDTH + MLA_Q_RANK]
    q = _dot(_rms(c_q, qg_ref[...]).astype(BF16), wuq_ref[...])
    qc, qs = qc_ref[...], qs_ref[...]
    partner = MLA_HEADS * HEAD_LANES
    for h in range(MLA_HEADS):
        lanes = slice(h * HEAD_LANES, (h + 1) * HEAD_LANES)
        lanes_p = slice(partner + h * HEAD_LANES, partner + (h + 1) * HEAD_LANES)
        q_ref[0, :, lanes] = (q[:, lanes] * qc + q[:, lanes_p] * qs).astype(q_ref.dtype)

    o_kv = POOL_WIDTH + MLA_Q_RANK
    c_kv = cur_ref[0, :, o_kv:o_kv + MLA_KV_RANK]
    kvn = _rms(c_kv, kvg_ref[...]).astype(BF16)
    k_nope = _dot(kvn, wuk_ref[...])
    o_kr = o_kv + MLA_KV_RANK
    k_rope = (cur_ref[0, :, o_kr:o_kr + HEAD_LANES] * kc_ref[...]
              + cur_ref[0, :, o_kr + HEAD_LANES:o_kr + 2 * HEAD_LANES] * ks_ref[...])
    for h in range(MLA_HEADS):
        lanes = slice(h * HEAD_LANES, (h + 1) * HEAD_LANES)
        k_ref[0, :, lanes] = (k_nope[:, lanes] + k_rope).astype(k_ref.dtype)
    vt_ref[0] = _dot_nt(wuvt_ref[...], kvn).astype(vt_ref.dtype)


def _mla_prep(x, norm_g, w_in, pool_w, pool_scale, qg, wuq, kvg, wuk, wuvt, q_tabs, k_tabs, ts):
    b, s, d = x.shape
    hb = ts // POOL_HALO
    n_halo = s // POOL_HALO
    tab_spec = pl.BlockSpec((ts, HEAD_LANES), lambda bi, i: (i, 0))
    kern = functools.partial(_mla_prep_kernel, ts=ts, seq=s)
    consts = [norm_g, w_in, pool_w, pool_scale, qg, wuq, kvg, wuk, wuvt]
    row_blk = lambda w: pl.BlockSpec((1, ts, w), lambda bi, i: (bi, i, 0))
    return pl.pallas_call(
        kern,
        grid=(b, s // ts),
        in_specs=[
            row_blk(d),
            pl.BlockSpec((1, POOL_HALO, d), lambda bi, i: (bi, jnp.maximum(i * hb - 1, 0), 0)),
            pl.BlockSpec((1, POOL_HALO, d),
                         lambda bi, i: (bi, jnp.minimum((i + 1) * hb, n_halo - 1), 0)),
        ] + [_const_spec(c.shape) for c in consts] + [tab_spec] * 4,
        out_specs=[
            row_blk(POOL_WIDTH),
            row_blk(MLA_HEADS * HEAD_LANES),
            row_blk(MLA_HEADS * HEAD_LANES),
            pl.BlockSpec((1, MLA_HEADS * MLA_V, ts), lambda bi, i: (bi, 0, i)),
        ],
        out_shape=[
            jax.ShapeDtypeStruct((b, s, POOL_WIDTH), BF16),
            jax.ShapeDtypeStruct((b, s, MLA_HEADS * HEAD_LANES), BF16),
            jax.ShapeDtypeStruct((b, s, MLA_HEADS * HEAD_LANES), BF16),
            jax.ShapeDtypeStruct((b, MLA_HEADS * MLA_V, s), BF16),
        ],
        scratch_shapes=[pltpu.VMEM((1, ts, PROJ_WIDTH), F32),
                        pltpu.VMEM((ts + 2 * POOL_HALO, POOL_WIDTH), F32)],
        compiler_params=_params(2),
        name="mla_prep",
    )(x, x, x, *consts, *q_tabs, *k_tabs)


def _ones_rows(cols):
    row = lax.broadcasted_iota(jnp.int32, (BF16_ROWS, cols), 0)
    return jnp.where(row == 0, 1.0, 0.0).astype(BF16)


SWEEP_ROWS = 16


def _attend_tiles(n_tiles, n_items, n_keys, scores, values, finish, s_ref, e_ref, extra=None):
    n_chunks = n_keys // KEY_CHUNK
    tq = s_ref.shape[-1]
    assert n_items % 2 == 0 and n_items >= 2 and n_chunks % 2 == 0

    def park(t, i, c):
        s_ref[i % 2, c * KEY_CHUNK:(c + 1) * KEY_CHUNK, :] = scores(t, i, c)

    def logits(i, c, r):
        blk = s_ref[i % 2, c * KEY_CHUNK + r:c * KEY_CHUNK + r + SWEEP_ROWS, :]
        add = None if extra is None else extra(i % n_items, c, r)
        return blk if add is None else blk + add

    def block_max(i, c, r, m8):
        blk = logits(i, c, r).reshape(SWEEP_ROWS // SUBLANES, SUBLANES, tq)
        for part in range(SWEEP_ROWS // SUBLANES):
            m8 = blk[part] if m8 is None else jnp.maximum(m8, blk[part])
        return m8

    m8 = None
    for c in range(n_chunks):
        park(0, 0, c)
    for c in range(n_chunks):
        for r in range(0, KEY_CHUNK, SWEEP_ROWS):
            m8 = block_max(0, c, r, m8)
    park(0, 1, 0)

    def tile(t, m8):
        t_next = jnp.minimum(t + 1, n_tiles - 1)
        accs = [None] * n_items

        def value_matmul(pos):
            i, c = divmod(pos, n_chunks)
            pv = _dot(values(t, i, c), e_ref[pos % 2])
            accs[i] = pv if accs[i] is None else accs[i] + pv

        for i in range(n_items):
            m = jnp.broadcast_to(jnp.max(m8, axis=0, keepdims=True), (SWEEP_ROWS, tq))
            m8 = None
            for c in range(n_chunks):
                pos = i * n_chunks + c
                ahead_i, ahead_c = divmod(pos + n_chunks + 1, n_chunks)
                if ahead_i < n_items:
                    park(t, ahead_i, ahead_c)
                else:
                    park(t_next, ahead_i - n_items, ahead_c)
                for r in range(0, KEY_CHUNK, SWEEP_ROWS):
                    e_ref[pos % 2, r:r + SWEEP_ROWS, :] = (
                        jnp.exp2(logits(i, c, r) - m).astype(BF16))
                    m8 = block_max(i + 1, c, r, m8)
                if pos > 0:
                    value_matmul(pos - 1)
        value_matmul(n_items * n_chunks - 1)
        finish(t, accs)
        return m8

    lax.fori_loop(0, n_tiles, tile, m8)


def _mla_attn_kernel(q_ref, k_ref, vt_ref, o_ref, s_ref, e_ref):
    tq = s_ref.shape[-1]
    ones = _ones_rows(KEY_CHUNK)

    def tile_rows(t):
        return pl.ds(pl.multiple_of(t * tq, tq), tq)

    q_t = {}

    def scores(t, j, c):
        lanes = slice(j * HEAD_LANES, (j + 1) * HEAD_LANES)
        keys = slice(c * KEY_CHUNK, (c + 1) * KEY_CHUNK)
        if (id(t), j) not in q_t:
            q_t[id(t), j] = q_ref[0, tile_rows(t), lanes].T
        return _dot(k_ref[0, keys, lanes], q_t[id(t), j])

    def values(t, j, c):
        vt = vt_ref[0, j * MLA_V:(j + 1) * MLA_V, c * KEY_CHUNK:(c + 1) * KEY_CHUNK]
        return jnp.concatenate([vt, ones], axis=0)

    def finish(t, accs):
        outs = [a[:MLA_V] * (1.0 / a[MLA_V:MLA_V + 1]) for a in accs]
        o_ref[0, tile_rows(t), :] = jnp.concatenate(outs, axis=0).T.astype(o_ref.dtype)

    _attend_tiles(q_ref.shape[1] // tq, MLA_GROUP, k_ref.shape[1], scores, values, finish,
                  s_ref, e_ref)


def _mla_attn(q, k, vt):
    b, s, _ = q.shape
    tq, g = ATTN_TQ, MLA_GROUP
    return pl.pallas_call(
        _mla_attn_kernel,
        grid=(b, MLA_HEADS // g),
        in_specs=[
            pl.BlockSpec((1, s, g * HEAD_LANES), lambda bi, p: (bi, 0, p)),
            pl.BlockSpec((1, s, g * HEAD_LANES), lambda bi, p: (bi, 0, p)),
            pl.BlockSpec((1, g * MLA_V, s), lambda bi, p: (bi, p, 0)),
        ],
        out_specs=pl.BlockSpec((1, s, g * MLA_V), lambda bi, p: (bi, 0, p)),
        out_shape=jax.ShapeDtypeStruct((b, s, MLA_HEADS * MLA_V), BF16),
        scratch_shapes=[pltpu.VMEM((2, s, tq), F32), pltpu.VMEM((2, KEY_CHUNK, tq), BF16)],
        compiler_params=_params(2),
        name="mla_attn",
    )(q, k, vt)


def _bf16_pieces(x):
    p1 = x.astype(BF16).astype(F32)
    p2 = (x - p1).astype(BF16).astype(F32)
    p3 = (x - p1 - p2).astype(BF16).astype(F32)
    return p1, p2, p3


def _piece_lanes(lane, pieces):
    sel = lane % 3
    return jnp.where(sel == 0, pieces[0], jnp.where(sel == 1, pieces[1], pieces[2]))


def _diff_attn_kernel(q_ref, k_ref, vt_ref, slope_ref, lq1_ref, lk1_ref, lq2_ref, lk2_ref,
                      g_ref, o_ref, kaug_ref, dbias_ref, s_ref, e_ref, *, lam_init):
    bi = pl.program_id(1)
    tq = s_ref.shape[-1]
    n_keys = k_ref.shape[1]
    n_chunks = n_keys // KEY_CHUNK
    n_tiles = q_ref.shape[1] // tq
    half = KEY_CHUNK // 2
    aug = HEAD_LANES

    slopes = [slope_ref[g, :, 0:1] for g in range(DIFF_GROUP)]
    pieces = [_bf16_pieces(c) for c in slopes]

    @pl.when(bi == 0)
    def _():
        j = lax.broadcasted_iota(jnp.int32, (n_keys, aug), 0)
        lane = lax.broadcasted_iota(jnp.int32, (n_keys, aug), 1)
        chunk = j // KEY_CHUNK
        dj = j % KEY_CHUNK
        lo = (dj % half).astype(F32)
        dk = lax.broadcasted_iota(jnp.int32, (KEY_CHUNK, tq), 0)
        dq = lax.broadcasted_iota(jnp.int32, (KEY_CHUNK, tq), 1)
        dist = jnp.abs(dq - dk).astype(F32)
        for g in range(DIFF_GROUP):
            neg_c = -_piece_lanes(lane, pieces[g])
            dbias_ref[g] = -slopes[g] * dist
            for qi in range(n_tiles):
                side = jnp.sign(qi - chunk).astype(F32)
                hi = (dj - dj % half - (qi - chunk) * KEY_CHUNK).astype(F32)
                val = jnp.where(lane < 6, neg_c,
                                jnp.where(lane < 9, lo, jnp.where(lane < 12, hi, 0.0)))
                kaug_ref[qi, g, :, aug:] = (side * val).astype(BF16)

    lam = (jnp.exp(jnp.sum(lq1_ref[...] * lk1_ref[...], axis=-1, keepdims=True))
           - jnp.exp(jnp.sum(lq2_ref[...] * lk2_ref[...], axis=-1, keepdims=True))
           + lam_init)

    lane = lax.broadcasted_iota(jnp.int32, (tq, HEAD_LANES), 1)
    di = lax.broadcasted_iota(jnp.int32, (tq, aug), 0)
    q_lo = (di % half).astype(F32)
    q_hi = (di - di % half).astype(F32)
    q_augs = []
    for g in range(DIFF_GROUP):
        k = k_ref[0, :, g * HEAD_LANES:(g + 1) * HEAD_LANES]
        for qi in range(n_tiles):
            kaug_ref[qi, g, :, :aug] = k
        c_lanes = _piece_lanes(lane, pieces[g])
        q_augs.append(jnp.where(lane < 3, q_lo, jnp.where(lane < 6, q_hi,
                                                           jnp.where(lane < 12, c_lanes, 0.0))))
    ones = _ones_rows(KEY_CHUNK)
    q_maps = {}

    def tile_rows(t):
        return pl.ds(pl.multiple_of(t * tq, tq), tq)

    def chunk_of(t, c):
        return (t + c) % n_chunks

    def scores(t, i, c):
        g, m = divmod(i, 2)
        if (id(t), i) not in q_maps:
            q = q_ref[0, tile_rows(t), g * HEAD_LANES:(g + 1) * HEAD_LANES]
            in_map = (lane >= m * DIFF_HEAD_DIM) & (lane < (m + 1) * DIFF_HEAD_DIM)
            q_m = jnp.where(in_map, q, jnp.zeros_like(q))
            q_cat = jnp.concatenate([q_m.astype(F32), q_augs[g]], axis=1)
            q_maps[id(t), i] = q_cat.T.astype(BF16)
        rows = pl.ds(pl.multiple_of(chunk_of(t, c) * KEY_CHUNK, KEY_CHUNK), KEY_CHUNK)
        return _dot(kaug_ref[t, g, rows, :], q_maps[id(t), i])

    def values(t, i, c):
        g = i // 2
        vt = vt_ref[0, chunk_of(t, c), g * HEAD_LANES:(g + 1) * HEAD_LANES, :]
        return jnp.concatenate([vt, ones], axis=0)

    def diagonal_bias(i, c, r):
        return dbias_ref[i // 2, r:r + SWEEP_ROWS, :] if c == 0 else None

    def finish(t, accs):
        d = HEAD_LANES
        for g in range(DIFF_GROUP):
            a0, a1 = accs[2 * g], accs[2 * g + 1]
            o_t = a0[:d] * (1.0 / a0[d:d + 1]) - a1[:d] * (lam / a1[d:d + 1])
            o = _rms(o_t.T, g_ref[...]) * (1.0 - lam_init)
            o_ref[0, tile_rows(t), g * d:(g + 1) * d] = o.astype(o_ref.dtype)

    _attend_tiles(n_tiles, 2 * DIFF_GROUP, n_keys, scores, values, finish, s_ref, e_ref,
                  diagonal_bias)


def _diff_attn(qk, vt, slopes, lq1, lk1, lq2, lk2, subln_g, lam_init):
    b, s, _ = qk.shape
    tq, g = ATTN_TQ, DIFF_GROUP
    n_groups = DIFF_HEADS // g
    n_chunks = s // KEY_CHUNK
    assert tq == KEY_CHUNK and vt.shape == (b, n_chunks, DIFF_HEADS * HEAD_LANES, KEY_CHUNK)
    kern = functools.partial(_diff_attn_kernel, lam_init=lam_init)
    vec = lambda a: _const_spec(a.shape)
    return pl.pallas_call(
        kern,
        grid=(n_groups, b),
        in_specs=[
            pl.BlockSpec((1, s, g * HEAD_LANES), lambda hi, bi: (bi, 0, hi)),
            pl.BlockSpec((1, s, g * HEAD_LANES), lambda hi, bi: (bi, 0, n_groups + hi)),
            pl.BlockSpec((1, n_chunks, g * HEAD_LANES, KEY_CHUNK), lambda hi, bi: (bi, 0, hi, 0)),
            pl.BlockSpec((g, 1, HEAD_LANES), lambda hi, bi: (hi, 0, 0)),
            vec(lq1), vec(lk1), vec(lq2), vec(lk2), vec(subln_g),
        ],
        out_specs=pl.BlockSpec((1, s, g * HEAD_LANES), lambda hi, bi: (bi, 0, hi)),
        out_shape=jax.ShapeDtypeStruct((b, s, DIFF_HEADS * HEAD_LANES), BF16),
        scratch_shapes=[pltpu.VMEM((s // tq, g, s, 2 * HEAD_LANES), BF16),
                        pltpu.VMEM((g, KEY_CHUNK, tq), F32),
                        pltpu.VMEM((2, s, tq), F32),
                        pltpu.VMEM((2, KEY_CHUNK, tq), BF16)],
        compiler_params=_params(2),
        name="diff_attn",
    )(qk, qk, vt, slopes, lq1, lk1, lq2, lk2, subln_g)


def _post_kernel(*refs, n_mix, final_norm):
    h_ref = refs[0]
    mix_refs = refs[1:1 + n_mix]
    wo_refs = refs[1 + n_mix:1 + 2 * n_mix]
    ng_ref, wg_ref, wu_ref, wd_ref = refs[1 + 2 * n_mix:5 + 2 * n_mix]
    rest = refs[5 + 2 * n_mix:]
    fg_ref = rest[0] if final_norm else None
    o_ref = rest[-1]

    h = h_ref[...]
    for m_ref, w_ref in zip(mix_refs, wo_refs):
        h = h + _dot(m_ref[...], w_ref[...])
    hn = _rms(h, ng_ref[...]).astype(BF16)
    gate = _dot(hn, wg_ref[...])
    up = _dot(hn, wu_ref[...])
    act = gate * jax.nn.sigmoid(gate) * up
    h = h + _dot(act.astype(BF16), wd_ref[...])
    if final_norm:
        h = _rms(h, fg_ref[...])
    o_ref[...] = h


def _post(h, mixes, w_outs, norm_g, w_gate, w_up, w_down, final_g, tm):
    n, d = h.shape
    n_mix = len(mixes)
    final_norm = final_g is not None
    kern = functools.partial(_post_kernel, n_mix=n_mix, final_norm=final_norm)
    row_spec = lambda a: pl.BlockSpec((tm, a.shape[1]), lambda i: (i, 0))
    consts = list(w_outs) + [norm_g, w_gate, w_up, w_down] + ([final_g] if final_norm else [])
    return pl.pallas_call(
        kern,
        grid=(n // tm,),
        in_specs=[row_spec(h)] + [row_spec(m) for m in mixes]
                 + [_const_spec(c.shape) for c in consts],
        out_specs=pl.BlockSpec((tm, d), lambda i: (i, 0)),
        out_shape=jax.ShapeDtypeStruct((n, d), F32),
        compiler_params=_params(1),
        name="post_final" if final_norm else "post",
    )(h, *mixes, *consts)


def _rope_head_blocks(w_rope, lead):
    half = MLA_ROPE // 2
    tail = HEAD_LANES - lead - MLA_ROPE
    widths = [(0, 0)] * (w_rope.ndim - 1) + [(lead, tail)]
    swapped = jnp.concatenate([w_rope[..., half:], w_rope[..., :half]], axis=-1)
    return jnp.pad(w_rope, widths), jnp.pad(swapped, widths)


def _hybrid_weights(w_in, w_uq, w_ukv):
    o1 = POOL_WIDTH + MLA_Q_RANK + MLA_KV_RANK
    k_r, k_r_sw = _rope_head_blocks(w_in[:, o1:], MLA_NOPE)
    w_in_p = jnp.concatenate([w_in[:, :o1], k_r, k_r_sw], axis=1).astype(BF16)
    r = w_uq.shape[0]
    wq = w_uq.reshape(r, MLA_HEADS, MLA_NOPE + MLA_ROPE)
    q_r, q_r_sw = _rope_head_blocks(wq[:, :, MLA_NOPE:], MLA_NOPE)
    nope = jnp.pad(wq[:, :, :MLA_NOPE], ((0, 0), (0, 0), (0, HEAD_LANES - MLA_NOPE)))
    wq = jnp.concatenate([(nope + q_r).reshape(r, -1), q_r_sw.reshape(r, -1)],
                         axis=1).astype(BF16)
    wkv = w_ukv.reshape(r, MLA_HEADS, MLA_NOPE + MLA_V)
    wk = jnp.pad(wkv[:, :, :MLA_NOPE], ((0, 0), (0, 0), (0, HEAD_LANES - MLA_NOPE)))
    wk = wk.reshape(r, MLA_HEADS * HEAD_LANES).astype(BF16)
    wvt = wkv[:, :, MLA_NOPE:].reshape(r, MLA_HEADS * MLA_V).T.astype(BF16)
    return w_in_p, wq, wk, wvt


def _rope_tables(seq, scale):
    half = MLA_ROPE // 2
    inv = ROPE_THETA ** (-jnp.arange(0, MLA_ROPE, 2, dtype=F32) / MLA_ROPE)
    ang = jnp.arange(seq, dtype=F32)[:, None] * inv[None, :]
    cos, sin = jnp.cos(ang), jnp.sin(ang)
    z = lambda w: jnp.zeros((seq, w), F32)
    tail = HEAD_LANES - MLA_NOPE - MLA_ROPE
    c = jnp.concatenate([jnp.ones((seq, MLA_NOPE), F32), cos, cos, z(tail)], axis=1)
    sn = jnp.concatenate([z(MLA_NOPE), -sin, sin, z(tail)], axis=1)
    return c * scale, sn * scale


def kernel(x, attn_norm_g, ffn_norm_g, final_norm_g, hyb_w_in, pool_w, pool_scale,
           mla_q_norm_g, mla_w_uq, mla_kv_norm_g, mla_w_ukv, hyb_w_out,
           diff_w_qkv, diff_lambda_q1, diff_lambda_k1, diff_lambda_q2, diff_lambda_k2,
           diff_subln_g, diff_w_out, ffn_w_gate, ffn_w_up, ffn_w_down):
    b, s, d = x.shape
    n = b * s
    row = lambda a: a.reshape(1, -1)
    h = x.reshape(n, d)

    w_in_p, wq_p, wk_p, wvt_p = _hybrid_weights(hyb_w_in[0], mla_w_uq[0], mla_w_ukv[0])
    q_tabs = _rope_tables(s, float((MLA_NOPE + MLA_ROPE) ** -0.5 * LOG2E))
    k_tabs = _rope_tables(s, 1.0)
    a_out, q, k, vt = _mla_prep(
        x, row(attn_norm_g[0]), w_in_p, pool_w[0].astype(BF16), row(pool_scale[0]),
        row(mla_q_norm_g[0]), wq_p, row(mla_kv_norm_g[0]), wk_p, wvt_p, q_tabs, k_tabs, 512)
    b_out = _mla_attn(q, k, vt)
    w_out0 = hyb_w_out[0].astype(BF16)
    h = _post(h, [a_out.reshape(n, -1), b_out.reshape(n, -1)],
              [w_out0[:POOL_WIDTH], w_out0[POOL_WIDTH:]],
              row(ffn_norm_g[0]), ffn_w_gate[0].astype(BF16), ffn_w_up[0].astype(BF16),
              ffn_w_down[0].astype(BF16), None, 512)

    qk_w = DIFF_HEADS * 2 * DIFF_HEAD_DIM
    w_qkv = diff_w_qkv[0]
    w_qk = jnp.concatenate([w_qkv[:, :qk_w] * (DIFF_HEAD_DIM ** -0.5 * LOG2E),
                            w_qkv[:, qk_w:2 * qk_w]], axis=1).astype(BF16)
    w_vt = w_qkv[:, 2 * qk_w:].T.astype(BF16)
    qk, vt = _norm_qkv(h.reshape(b, s, d), row(attn_norm_g[1]), w_qk, w_vt, KEY_CHUNK)
    slopes = 2.0 ** (-8.0 * jnp.arange(1, DIFF_HEADS + 1, dtype=F32) / DIFF_HEADS)
    slopes2 = jnp.broadcast_to((LOG2E * slopes)[:, None, None], (DIFF_HEADS, 1, HEAD_LANES))
    lam_init = 0.8 - 0.6 * math.exp(-0.3 * 1)
    o = _diff_attn(qk, vt, slopes2, row(diff_lambda_q1[0]),
                   row(diff_lambda_k1[0]), row(diff_lambda_q2[0]), row(diff_lambda_k2[0]),
                   row(diff_subln_g[0]), lam_init)
    h = _post(h, [o.reshape(n, -1)], [diff_w_out[0].astype(BF16)],
              row(ffn_norm_g[1]), ffn_w_gate[1].astype(BF16), ffn_w_up[1].astype(BF16),
              ffn_w_down[1].astype(BF16), row(final_norm_g), 512)
    return h.reshape(b, s, d)
```

```python
import functools
import math

import jax
import jax.numpy as jnp
from jax import lax
from jax.experimental import pallas as pl
from jax.experimental.pallas import tpu as pltpu

D_MODEL = 1024
EPS = 1e-6
LOG2E = math.log2(math.e)

POOL_WINDOWS = (2, 4, 8, 16)
POOL_GROUP_DIM = 128
POOL_WIDTH = 512
POOL_HALO = 8

MLA_HEADS = 8
MLA_NOPE = 64
MLA_ROPE = 32
MLA_V = 64
MLA_Q_RANK = 256
MLA_KV_RANK = 256
ROPE_THETA = 10000.0
HEAD_LANES = 128
PROJ_WIDTH = POOL_WIDTH + MLA_Q_RANK + MLA_KV_RANK + 2 * HEAD_LANES

DIFF_HEADS = 8
DIFF_HEAD_DIM = 64
D_FF = 2816

SUBLANES = 8
BF16_ROWS = 16
KEY_CHUNK = 512
ATTN_TQ = 512
MLA_GROUP = 8
DIFF_GROUP = 4

VMEM_LIMIT = 56 * 1024 * 1024

F32 = jnp.float32
BF16 = jnp.bfloat16


def _rms(x, g):
    return x * lax.rsqrt(jnp.mean(x * x, axis=-1, keepdims=True) + EPS) * g


def _dot(a, b):
    return jnp.dot(a, b, preferred_element_type=F32)


def _dot_nt(a, b):
    return lax.dot_general(a, b, (((1,), (1,)), ((), ())), preferred_element_type=F32)


def _const_spec(shape):
    nd = len(shape)
    return pl.BlockSpec(shape, lambda *_: (0,) * nd, pipeline_mode=pl.Buffered(1))


def _params(n_axes):
    return pltpu.CompilerParams(dimension_semantics=("arbitrary",) * n_axes,
                                vmem_limit_bytes=VMEM_LIMIT)


def _norm_qkv_kernel(x_ref, g_ref, wqk_ref, wvt_ref, qk_ref, vt_ref):
    y = _rms(x_ref[0], g_ref[...]).astype(BF16)
    qk_ref[0] = _dot(y, wqk_ref[...]).astype(qk_ref.dtype)
    vt_ref[0, 0] = _dot_nt(wvt_ref[...], y).astype(vt_ref.dtype)


def _norm_qkv(x, g, w_qk, w_vt, tm):
    b, s, d = x.shape
    return pl.pallas_call(
        _norm_qkv_kernel,
        grid=(b, s // tm),
        in_specs=[pl.BlockSpec((1, tm, d), lambda bi, i: (bi, i, 0)),
                  _const_spec(g.shape), _const_spec(w_qk.shape), _const_spec(w_vt.shape)],
        out_specs=[pl.BlockSpec((1, tm, w_qk.shape[1]), lambda bi, i: (bi, i, 0)),
                   pl.BlockSpec((1, 1, w_vt.shape[0], tm), lambda bi, i: (bi, i, 0, 0))],
        out_shape=[jax.ShapeDtypeStruct((b, s, w_qk.shape[1]), BF16),
                   jax.ShapeDtypeStruct((b, s // tm, w_vt.shape[0], tm), BF16)],
        compiler_params=_params(2),
        name="norm_qkv",
    )(x, g, w_qk, w_vt)


def _mla_prep_kernel(x_ref, xprev_ref, xnext_ref, ng_ref, win_ref, pool_w_ref, pool_scale_ref,
                     qg_ref, wuq_ref, kvg_ref, wuk_ref, wuvt_ref,
                     qc_ref, qs_ref, kc_ref, ks_ref,
                     a_ref, q_ref, k_ref, vt_ref, cur_ref, pad_ref, *, ts, seq):
    i = pl.program_id(1)
    n_tiles = pl.num_programs(1)

    cur_ref[0] = _dot(_rms(x_ref[0], ng_ref[...]).astype(BF16), win_ref[...])
    x_halo = jnp.concatenate([xprev_ref[0], xnext_ref[0]], axis=0)
    u_halo = _dot(_rms(x_halo, ng_ref[...]).astype(BF16), win_ref[:, 0:POOL_WIDTH])

    pad_ref[0:POOL_HALO, :] = jnp.where(i > 0, u_halo[:POOL_HALO], 0.0)
    pad_ref[POOL_HALO:POOL_HALO + ts, :] = cur_ref[0, :, 0:POOL_WIDTH]
    pad_ref[POOL_HALO + ts:2 * POOL_HALO + ts, :] = jnp.where(i < n_tiles - 1,
                                                              u_halo[POOL_HALO:], 0.0)
    t = i * ts + lax.broadcasted_iota(jnp.int32, (ts, 1), 0)
    for g, w in enumerate(POOL_WINDOWS):
        lanes = slice(g * POOL_GROUP_DIM, (g + 1) * POOL_GROUP_DIM)
        acc = None
        for off in range(-(w // 2), w // 2):
            piece = pad_ref[POOL_HALO + off:POOL_HALO + off + ts, lanes]
            acc = piece if acc is None else acc + piece
        cnt = jnp.minimum(t + w // 2, seq) - jnp.maximum(t - w // 2, 0)
        inv_cnt = 1.0 / cnt.astype(F32)
        pooled = acc * inv_cnt - cur_ref[0, :, lanes]
        a_g = _dot(pooled.astype(BF16), pool_w_ref[g]) * pool_scale_ref[:, lanes]
        a_ref[0, :, lanes] = a_g.astype(a_ref.dtype)

    c_q = cur_ref[0, :, POOL_WIDTH:POOL_WIDTH + MLA_Q_RANK]
    q = _dot(_rms(c_q, qg_ref[...]).astype(BF16), wuq_ref[...])
    qc, qs = qc_ref[...], qs_ref[...]
    partner = MLA_HEADS * HEAD_LANES
    for h in range(MLA_HEADS):
        lanes = slice(h * HEAD_LANES, (h + 1) * HEAD_LANES)
        lanes_p = slice(partner + h * HEAD_LANES, partner + (h + 1) * HEAD_LANES)
        q_ref[0, :, lanes] = (q[:, lanes] * qc + q[:, lanes_p] * qs).astype(q_ref.dtype)

    o_kv = POOL_WIDTH + MLA_Q_RANK
    c_kv = cur_ref[0, :, o_kv:o_kv + MLA_KV_RANK]
    kvn = _rms(c_kv, kvg_ref[...]).astype(BF16)
    k_nope = _dot(kvn, wuk_ref[...])
    o_kr = o_kv + MLA_KV_RANK
    k_rope = (cur_ref[0, :, o_kr:o_kr + HEAD_LANES] * kc_ref[...]
              + cur_ref[0, :, o_kr + HEAD_LANES:o_kr + 2 * HEAD_LANES] * ks_ref[...])
    for h in range(MLA_HEADS):
        lanes = slice(h * HEAD_LANES, (h + 1) * HEAD_LANES)
        k_ref[0, :, lanes] = (k_nope[:, lanes] + k_rope).astype(k_ref.dtype)
    vt_ref[0] = _dot_nt(wuvt_ref[...], kvn).astype(vt_ref.dtype)


def _mla_prep(x, norm_g, w_in, pool_w, pool_scale, qg, wuq, kvg, wuk, wuvt, q_tabs, k_tabs, ts):
    b, s, d = x.shape
    hb = ts // POOL_HALO
    n_halo = s // POOL_HALO
    tab_spec = pl.BlockSpec((ts, HEAD_LANES), lambda bi, i: (i, 0))
    kern = functools.partial(_mla_prep_kernel, ts=ts, seq=s)
    consts = [norm_g, w_in, pool_w, pool_scale, qg, wuq, kvg, wuk, wuvt]
    row_blk = lambda w: pl.BlockSpec((1, ts, w), lambda bi, i: (bi, i, 0))
    return pl.pallas_call(
        kern,
        grid=(b, s // ts),
        in_specs=[
            row_blk(d),
            pl.BlockSpec((1, POOL_HALO, d), lambda bi, i: (bi, jnp.maximum(i * hb - 1, 0), 0)),
            pl.BlockSpec((1, POOL_HALO, d),
                         lambda bi, i: (bi, jnp.minimum((i + 1) * hb, n_halo - 1), 0)),
        ] + [_const_spec(c.shape) for c in consts] + [tab_spec] * 4,
        out_specs=[
            row_blk(POOL_WIDTH),
            row_blk(MLA_HEADS * HEAD_LANES),
            row_blk(MLA_HEADS * HEAD_LANES),
            pl.BlockSpec((1, MLA_HEADS * MLA_V, ts), lambda bi, i: (bi, 0, i)),
        ],
        out_shape=[
            jax.ShapeDtypeStruct((b, s, POOL_WIDTH), BF16),
            jax.ShapeDtypeStruct((b, s, MLA_HEADS * HEAD_LANES), BF16),
            jax.ShapeDtypeStruct((b, s, MLA_HEADS * HEAD_LANES), BF16),
            jax.ShapeDtypeStruct((b, MLA_HEADS * MLA_V, s), BF16),
        ],
        scratch_shapes=[pltpu.VMEM((1, ts, PROJ_WIDTH), F32),
                        pltpu.VMEM((ts + 2 * POOL_HALO, POOL_WIDTH), F32)],
        compiler_params=_params(2),
        name="mla_prep",
    )(x, x, x, *consts, *q_tabs, *k_tabs)


def _ones_rows(cols):
    row = lax.broadcasted_iota(jnp.int32, (BF16_ROWS, cols), 0)
    return jnp.where(row == 0, 1.0, 0.0).astype(BF16)


SWEEP_ROWS = 16


def _attend_tiles(n_tiles, n_items, n_keys, scores, values, finish, s_ref, e_ref, extra=None):
    n_chunks = n_keys // KEY_CHUNK
    tq = s_ref.shape[-1]
    assert n_items % 2 == 0 and n_items >= 2 and n_chunks % 2 == 0

    def park(t, i, c):
        s_ref[i % 2, c * KEY_CHUNK:(c + 1) * KEY_CHUNK, :] = scores(t, i, c)

    def logits(i, c, r):
        blk = s_ref[i % 2, c * KEY_CHUNK + r:c * KEY_CHUNK + r + SWEEP_ROWS, :]
        add = None if extra is None else extra(i % n_items, c, r)
        return blk if add is None else blk + add

    def block_max(i, c, r, m8):
        blk = logits(i, c, r).reshape(SWEEP_ROWS // SUBLANES, SUBLANES, tq)
        for part in range(SWEEP_ROWS // SUBLANES):
            m8 = blk[part] if m8 is None else jnp.maximum(m8, blk[part])
        return m8

    m8 = None
    for c in range(n_chunks):
        park(0, 0, c)
    for c in range(n_chunks):
        for r in range(0, KEY_CHUNK, SWEEP_ROWS):
            m8 = block_max(0, c, r, m8)
    park(0, 1, 0)

    def tile(t, m8):
        t_next = jnp.minimum(t + 1, n_tiles - 1)
        accs = [None] * n_items

        def value_matmul(pos):
            i, c = divmod(pos, n_chunks)
            pv = _dot(values(t, i, c), e_ref[pos % 2])
            accs[i] = pv if accs[i] is None else accs[i] + pv

        for i in range(n_items):
            m = jnp.broadcast_to(jnp.max(m8, axis=0, keepdims=True), (SWEEP_ROWS, tq))
            m8 = None
            for c in range(n_chunks):
                pos = i * n_chunks + c
                ahead_i, ahead_c = divmod(pos + n_chunks + 1, n_chunks)
                if ahead_i < n_items:
                    park(t, ahead_i, ahead_c)
                else:
                    park(t_next, ahead_i - n_items, ahead_c)
                for r in range(0, KEY_CHUNK, SWEEP_ROWS):
                    e_ref[pos % 2, r:r + SWEEP_ROWS, :] = (
                        jnp.exp2(logits(i, c, r) - m).astype(BF16))
                    m8 = block_max(i + 1, c, r, m8)
                if pos > 0:
                    value_matmul(pos - 1)
        value_matmul(n_items * n_chunks - 1)
        finish(t, accs)
        return m8

    lax.fori_loop(0, n_tiles, tile, m8)


def _mla_attn_kernel(q_ref, k_ref, vt_ref, o_ref, s_ref, e_ref):
    tq = s_ref.shape[-1]
    ones = _ones_rows(KEY_CHUNK)

    def tile_rows(t):
        return pl.ds(pl.multiple_of(t * tq, tq), tq)

    q_t = {}

    def scores(t, j, c):
        lanes = slice(j * HEAD_LANES, (j + 1) * HEAD_LANES)
        keys = slice(c * KEY_CHUNK, (c + 1) * KEY_CHUNK)
        if (id(t), j) not in q_t:
            q_t[id(t), j] = q_ref[0, tile_rows(t), lanes].T
        return _dot(k_ref[0, keys, lanes], q_t[id(t), j])

    def values(t, j, c):
        vt = vt_ref[0, j * MLA_V:(j + 1) * MLA_V, c * KEY_CHUNK:(c + 1) * KEY_CHUNK]
        return jnp.concatenate([vt, ones], axis=0)

    def finish(t, accs):
        outs = [a[:MLA_V] * (1.0 / a[MLA_V:MLA_V + 1]) for a in accs]
        o_ref[0, tile_rows(t), :] = jnp.concatenate(outs, axis=0).T.astype(o_ref.dtype)

    _attend_tiles(q_ref.shape[1] // tq, MLA_GROUP, k_ref.shape[1], scores, values, finish,
                  s_ref, e_ref)


def _mla_attn(q, k, vt):
    b, s, _ = q.shape
    tq, g = ATTN_TQ, MLA_GROUP
    return pl.pallas_call(
        _mla_attn_kernel,
        grid=(b, MLA_HEADS // g),
        in_specs=[
            pl.BlockSpec((1, s, g * HEAD_LANES), lambda bi, p: (bi, 0, p)),
            pl.BlockSpec((1, s, g * HEAD_LANES), lambda bi, p: (bi, 0, p)),
            pl.BlockSpec((1, g * MLA_V, s), lambda bi, p: (bi, p, 0)),
        ],
        out_specs=pl.BlockSpec((1, s, g * MLA_V), lambda bi, p: (bi, 0, p)),
        out_shape=jax.ShapeDtypeStruct((b, s, MLA_HEADS * MLA_V), BF16),
        scratch_shapes=[pltpu.VMEM((2, s, tq), F32), pltpu.VMEM((2, KEY_CHUNK, tq), BF16)],
        compiler_params=_params(2),
        name="mla_attn",
    )(q, k, vt)


def _bf16_pieces(x):
    p1 = x.astype(BF16).astype(F32)
    p2 = (x - p1).astype(BF16).astype(F32)
    p3 = (x - p1 - p2).astype(BF16).astype(F32)
    return p1, p2, p3


def _piece_lanes(lane, pieces):
    sel = lane % 3
    return jnp.where(sel == 0, pieces[0], jnp.where(sel == 1, pieces[1], pieces[2]))


def _diff_attn_kernel(q_ref, k_ref, vt_ref, slope_ref, lq1_ref, lk1_ref, lq2_ref, lk2_ref,
                      g_ref, o_ref, kaug_ref, dbias_ref, s_ref, e_ref, *, lam_init):
    bi = pl.program_id(1)
    tq = s_ref.shape[-1]
    n_keys = k_ref.shape[1]
    n_chunks = n_keys // KEY_CHUNK
    n_tiles = q_ref.shape[1] // tq
    half = KEY_CHUNK // 2
    aug = HEAD_LANES

    slopes = [slope_ref[g, :, 0:1] for g in range(DIFF_GROUP)]
    pieces = [_bf16_pieces(c) for c in slopes]

    @pl.when(bi == 0)
    def _():
        j = lax.broadcasted_iota(jnp.int32, (n_keys, aug), 0)
        lane = lax.broadcasted_iota(jnp.int32, (n_keys, aug), 1)
        chunk = j // KEY_CHUNK
        dj = j % KEY_CHUNK
        lo = (dj % half).astype(F32)
        dk = lax.broadcasted_iota(jnp.int32, (KEY_CHUNK, tq), 0)
        dq = lax.broadcasted_iota(jnp.int32, (KEY_CHUNK, tq), 1)
        dist = jnp.abs(dq - dk).astype(F32)
        for g in range(DIFF_GROUP):
            neg_c = -_piece_lanes(lane, pieces[g])
            dbias_ref[g] = -slopes[g] * dist
            for qi in range(n_tiles):
                side = jnp.sign(qi - chunk).astype(F32)
                hi = (dj - dj % half - (qi - chunk) * KEY_CHUNK).astype(F32)
                val = jnp.where(lane < 6, neg_c,
                                jnp.where(lane < 9, lo, jnp.where(lane < 12, hi, 0.0)))
                kaug_ref[qi, g, :, aug:] = (side * val).astype(BF16)

    lam = (jnp.exp(jnp.sum(lq1_ref[...] * lk1_ref[...], axis=-1, keepdims=True))
           - jnp.exp(jnp.sum(lq2_ref[...] * lk2_ref[...], axis=-1, keepdims=True))
           + lam_init)

    lane = lax.broadcasted_iota(jnp.int32, (tq, HEAD_LANES), 1)
    di = lax.broadcasted_iota(jnp.int32, (tq, aug), 0)
    q_lo = (di % half).astype(F32)
    q_hi = (di - di % half).astype(F32)
    q_augs = []
    for g in range(DIFF_GROUP):
        k = k_ref[0, :, g * HEAD_LANES:(g + 1) * HEAD_LANES]
        for qi in range(n_tiles):
            kaug_ref[qi, g, :, :aug] = k
        c_lanes = _piece_lanes(lane, pieces[g])
        q_augs.append(jnp.where(lane < 3, q_lo, jnp.where(lane < 6, q_hi,
                                                           jnp.where(lane < 12, c_lanes, 0.0))))
    ones = _ones_rows(KEY_CHUNK)
    q_maps = {}

    def tile_rows(t):
        return pl.ds(pl.multiple_of(t * tq, tq), tq)

    def chunk_of(t, c):
        return (t + c) % n_chunks

    def scores(t, i, c):
        g, m = divmod(i, 2)
        if (id(t), i) not in q_maps:
            q = q_ref[0, tile_rows(t), g * HEAD_LANES:(g + 1) * HEAD_LANES]
            in_map = (lane >= m * DIFF_HEAD_DIM) & (lane < (m + 1) * DIFF_HEAD_DIM)
            q_m = jnp.where(in_map, q, jnp.zeros_like(q))
            q_cat = jnp.concatenate([q_m.astype(F32), q_augs[g]], axis=1)
            q_maps[id(t), i] = q_cat.T.astype(BF16)
        rows = pl.ds(pl.multiple_of(chunk_of(t, c) * KEY_CHUNK, KEY_CHUNK), KEY_CHUNK)
        return _dot(kaug_ref[t, g, rows, :], q_maps[id(t), i])

    def values(t, i, c):
        g = i // 2
        vt = vt_ref[0, chunk_of(t, c), g * HEAD_LANES:(g + 1) * HEAD_LANES, :]
        return jnp.concatenate([vt, ones], axis=0)

    def diagonal_bias(i, c, r):
        return dbias_ref[i // 2, r:r + SWEEP_ROWS, :] if c == 0 else None

    def finish(t, accs):
        d = HEAD_LANES
        for g in range(DIFF_GROUP):
            a0, a1 = accs[2 * g], accs[2 * g + 1]
            o_t = a0[:d] * (1.0 / a0[d:d + 1]) - a1[:d] * (lam / a1[d:d + 1])
            o = _rms(o_t.T, g_ref[...]) * (1.0 - lam_init)
            o_ref[0, tile_rows(t), g * d:(g + 1) * d] = o.astype(o_ref.dtype)

    _attend_tiles(n_tiles, 2 * DIFF_GROUP, n_keys, scores, values, finish, s_ref, e_ref,
                  diagonal_bias)


def _diff_attn(qk, vt, slopes, lq1, lk1, lq2, lk2, subln_g, lam_init):
    b, s, _ = qk.shape
    tq, g = ATTN_TQ, DIFF_GROUP
    n_groups = DIFF_HEADS // g
    n_chunks = s // KEY_CHUNK
    assert tq == KEY_CHUNK and vt.shape == (b, n_chunks, DIFF_HEADS * HEAD_LANES, KEY_CHUNK)
    kern = functools.partial(_diff_attn_kernel, lam_init=lam_init)
    vec = lambda a: _const_spec(a.shape)
    return pl.pallas_call(
        kern,
        grid=(n_groups, b),
        in_specs=[
            pl.BlockSpec((1, s, g * HEAD_LANES), lambda hi, bi: (bi, 0, hi)),
            pl.BlockSpec((1, s, g * HEAD_LANES), lambda hi, bi: (bi, 0, n_groups + hi)),
            pl.BlockSpec((1, n_chunks, g * HEAD_LANES, KEY_CHUNK), lambda hi, bi: (bi, 0, hi, 0)),
            pl.BlockSpec((g, 1, HEAD_LANES), lambda hi, bi: (hi, 0, 0)),
            vec(lq1), vec(lk1), vec(lq2), vec(lk2), vec(subln_g),
        ],
        out_specs=pl.BlockSpec((1, s, g * HEAD_LANES), lambda hi, bi: (bi, 0, hi)),
        out_shape=jax.ShapeDtypeStruct((b, s, DIFF_HEADS * HEAD_LANES), BF16),
        scratch_shapes=[pltpu.VMEM((s // tq, g, s, 2 * HEAD_LANES), BF16),
                        pltpu.VMEM((g, KEY_CHUNK, tq), F32),
                        pltpu.VMEM((2, s, tq), F32),
                        pltpu.VMEM((2, KEY_CHUNK, tq), BF16)],
        compiler_params=_params(2),
        name="diff_attn",
    )(qk, qk, vt, slopes, lq1, lk1, lq2, lk2, subln_g)


def _post_kernel(*refs, n_mix, final_norm):
    h_ref = refs[0]
    mix_refs = refs[1:1 + n_mix]
    wo_refs = refs[1 + n_mix:1 + 2 * n_mix]
    ng_ref, wg_ref, wu_ref, wd_ref = refs[1 + 2 * n_mix:5 + 2 * n_mix]
    rest = refs[5 + 2 * n_mix:]
    fg_ref = rest[0] if final_norm else None
    o_ref = rest[-1]

    h = h_ref[...]
    for m_ref, w_ref in zip(mix_refs, wo_refs):
        h = h + _dot(m_ref[...], w_ref[...])
    hn = _rms(h, ng_ref[...]).astype(BF16)
    gate = _dot(hn, wg_ref[...])
    up = _dot(hn, wu_ref[...])
    act = gate * jax.nn.sigmoid(gate) * up
    h = h + _dot(act.astype(BF16), wd_ref[...])
    if final_norm:
        h = _rms(h, fg_ref[...])
    o_ref[...] = h


def _post(h, mixes, w_outs, norm_g, w_gate, w_up, w_down, final_g, tm):
    n, d = h.shape
    n_mix = len(mixes)
    final_norm = final_g is not None
    kern = functools.partial(_post_kernel, n_mix=n_mix, final_norm=final_norm)
    row_spec = lambda a: pl.BlockSpec((tm, a.shape[1]), lambda i: (i, 0))
    consts = list(w_outs) + [norm_g, w_gate, w_up, w_down] + ([final_g] if final_norm else [])
    return pl.pallas_call(
        kern,
        grid=(n // tm,),
        in_specs=[row_spec(h)] + [row_spec(m) for m in mixes]
                 + [_const_spec(c.shape) for c in consts],
        out_specs=pl.BlockSpec((tm, d), lambda i: (i, 0)),
        out_shape=jax.ShapeDtypeStruct((n, d), F32),
        compiler_params=_params(1),
        name="post_final" if final_norm else "post",
    )(h, *mixes, *consts)


def _rope_head_blocks(w_rope, lead):
    half = MLA_ROPE // 2
    tail = HEAD_LANES - lead - MLA_ROPE
    widths = [(0, 0)] * (w_rope.ndim - 1) + [(lead, tail)]
    swapped = jnp.concatenate([w_rope[..., half:], w_rope[..., :half]], axis=-1)
    return jnp.pad(w_rope, widths), jnp.pad(swapped, widths)


def _hybrid_weights(w_in, w_uq, w_ukv):
    o1 = POOL_WIDTH + MLA_Q_RANK + MLA_KV_RANK
    k_r, k_r_sw = _rope_head_blocks(w_in[:, o1:], MLA_NOPE)
    w_in_p = jnp.concatenate([w_in[:, :o1], k_r, k_r_sw], axis=1).astype(BF16)
    r = w_uq.shape[0]
    wq = w_uq.reshape(r, MLA_HEADS, MLA_NOPE + MLA_ROPE)
    q_r, q_r_sw = _rope_head_blocks(wq[:, :, MLA_NOPE:], MLA_NOPE)
    nope = jnp.pad(wq[:, :, :MLA_NOPE], ((0, 0), (0, 0), (0, HEAD_LANES - MLA_NOPE)))
    wq = jnp.concatenate([(nope + q_r).reshape(r, -1), q_r_sw.reshape(r, -1)],
                         axis=1).astype(BF16)
    wkv = w_ukv.reshape(r, MLA_HEADS, MLA_NOPE + MLA_V)
    wk = jnp.pad(wkv[:, :, :MLA_NOPE], ((0, 0), (0, 0), (0, HEAD_LANES - MLA_NOPE)))
    wk = wk.reshape(r, MLA_HEADS * HEAD_LANES).astype(BF16)
    wvt = wkv[:, :, MLA_NOPE:].reshape(r, MLA_HEADS * MLA_V).T.astype(BF16)
    return w_in_p, wq, wk, wvt


def _rope_tables(seq, scale):
    half = MLA_ROPE // 2
    inv = ROPE_THETA ** (-jnp.arange(0, MLA_ROPE, 2, dtype=F32) / MLA_ROPE)
    ang = jnp.arange(seq, dtype=F32)[:, None] * inv[None, :]
    cos, sin = jnp.cos(ang), jnp.sin(ang)
    z = lambda w: jnp.zeros((seq, w), F32)
    tail = HEAD_LANES - MLA_NOPE - MLA_ROPE
    c = jnp.concatenate([jnp.ones((seq, MLA_NOPE), F32), cos, cos, z(tail)], axis=1)
    sn = jnp.concatenate([z(MLA_NOPE), -sin, sin, z(tail)], axis=1)
    return c * scale, sn * scale


def kernel(x, attn_norm_g, ffn_norm_g, final_norm_g, hyb_w_in, pool_w, pool_scale,
           mla_q_norm_g, mla_w_uq, mla_kv_norm_g, mla_w_ukv, hyb_w_out,
           diff_w_qkv, diff_lambda_q1, diff_lambda_k1, diff_lambda_q2, diff_lambda_k2,
           diff_subln_g, diff_w_out, ffn_w_gate, ffn_w_up, ffn_w_down):
    b, s, d = x.shape
    n = b * s
    row = lambda a: a.reshape(1, -1)
    h = x.reshape(n, d)

    w_in_p, wq_p, wk_p, wvt_p = _hybrid_weights(hyb_w_in[0], mla_w_uq[0], mla_w_ukv[0])
    q_tabs = _rope_tables(s, float((MLA_NOPE + MLA_ROPE) ** -0.5 * LOG2E))
    k_tabs = _rope_tables(s, 1.0)
    a_out, q, k, vt = _mla_prep(
        x, row(attn_norm_g[0]), w_in_p, pool_w[0].astype(BF16), row(pool_scale[0]),
        row(mla_q_norm_g[0]), wq_p, row(mla_kv_norm_g[0]), wk_p, wvt_p, q_tabs, k_tabs, 1024)
    b_out = _mla_attn(q, k, vt)
    w_out0 = hyb_w_out[0].astype(BF16)
    h = _post(h, [a_out.reshape(n, -1), b_out.reshape(n, -1)],
              [w_out0[:POOL_WIDTH], w_out0[POOL_WIDTH:]],
              row(ffn_norm_g[0]), ffn_w_gate[0].astype(BF16), ffn_w_up[0].astype(BF16),
              ffn_w_down[0].astype(BF16), None, 512)

    qk_w = DIFF_HEADS * 2 * DIFF_HEAD_DIM
    w_qkv = diff_w_qkv[0]
    w_qk = jnp.concatenate([w_qkv[:, :qk_w] * (DIFF_HEAD_DIM ** -0.5 * LOG2E),
                            w_qkv[:, qk_w:2 * qk_w]], axis=1).astype(BF16)
    w_vt = w_qkv[:, 2 * qk_w:].T.astype(BF16)
    qk, vt = _norm_qkv(h.reshape(b, s, d), row(attn_norm_g[1]), w_qk, w_vt, KEY_CHUNK)
    slopes = 2.0 ** (-8.0 * jnp.arange(1, DIFF_HEADS + 1, dtype=F32) / DIFF_HEADS)
    slopes2 = jnp.broadcast_to((LOG2E * slopes)[:, None, None], (DIFF_HEADS, 1, HEAD_LANES))
    lam_init = 0.8 - 0.6 * math.exp(-0.3 * 1)
    o = _diff_attn(qk, vt, slopes2, row(diff_lambda_q1[0]),
                   row(diff_lambda_k1[0]), row(diff_lambda_q2[0]), row(diff_lambda_k2[0]),
                   row(diff_subln_g[0]), lam_init)
    h = _post(h, [o.reshape(n, -1)], [diff_w_out[0].astype(BF16)],
              row(ffn_norm_g[1]), ffn_w_gate[1].astype(BF16), ffn_w_up[1].astype(BF16),
              ffn_w_down[1].astype(BF16), row(final_norm_g), 512)
    return h.reshape(b, s, d)
```

```python
import functools
import math

import jax
import jax.numpy as jnp
from jax import lax
from jax.experimental import pallas as pl
from jax.experimental.pallas import tpu as pltpu

D_MODEL = 1024
EPS = 1e-6
LOG2E = math.log2(math.e)

POOL_WINDOWS = (2, 4, 8, 16)
POOL_GROUP_DIM = 128
POOL_WIDTH = 512
POOL_HALO = 8

MLA_HEADS = 8
MLA_NOPE = 64
MLA_ROPE = 32
MLA_V = 64
MLA_Q_RANK = 256
MLA_KV_RANK = 256
ROPE_THETA = 10000.0
HEAD_LANES = 128
PROJ_WIDTH = POOL_WIDTH + MLA_Q_RANK + MLA_KV_RANK + 2 * HEAD_LANES

DIFF_HEADS = 8
DIFF_HEAD_DIM = 64
D_FF = 2816

SUBLANES = 8
BF16_ROWS = 16
KEY_CHUNK = 512
ATTN_TQ = 512
MLA_GROUP = 8
DIFF_GROUP = 4

VMEM_LIMIT = 56 * 1024 * 1024

F32 = jnp.float32
BF16 = jnp.bfloat16


def _rms(x, g):
    return x * lax.rsqrt(jnp.mean(x * x, axis=-1, keepdims=True) + EPS) * g


def _dot(a, b):
    return jnp.dot(a, b, preferred_element_type=F32)


def _dot_nt(a, b):
    return lax.dot_general(a, b, (((1,), (1,)), ((), ())), preferred_element_type=F32)


def _const_spec(shape):
    nd = len(shape)
    return pl.BlockSpec(shape, lambda *_: (0,) * nd, pipeline_mode=pl.Buffered(1))


def _params(n_axes):
    return pltpu.CompilerParams(dimension_semantics=("arbitrary",) * n_axes,
                                vmem_limit_bytes=VMEM_LIMIT)


def _norm_qkv_kernel(x_ref, g_ref, wqk_ref, wvt_ref, qk_ref, vt_ref):
    y = _rms(x_ref[0], g_ref[...]).astype(BF16)
    qk_ref[0] = _dot(y, wqk_ref[...]).astype(qk_ref.dtype)
    vt_ref[0, 0] = _dot_nt(wvt_ref[...], y).astype(vt_ref.dtype)


def _norm_qkv(x, g, w_qk, w_vt, tm):
    b, s, d = x.shape
    return pl.pallas_call(
        _norm_qkv_kernel,
        grid=(b, s // tm),
        in_specs=[pl.BlockSpec((1, tm, d), lambda bi, i: (bi, i, 0)),
                  _const_spec(g.shape), _const_spec(w_qk.shape), _const_spec(w_vt.shape)],
        out_specs=[pl.BlockSpec((1, tm, w_qk.shape[1]), lambda bi, i: (bi, i, 0)),
                   pl.BlockSpec((1, 1, w_vt.shape[0], tm), lambda bi, i: (bi, i, 0, 0))],
        out_shape=[jax.ShapeDtypeStruct((b, s, w_qk.shape[1]), BF16),
                   jax.ShapeDtypeStruct((b, s // tm, w_vt.shape[0], tm), BF16)],
        compiler_params=_params(2),
        name="norm_qkv",
    )(x, g, w_qk, w_vt)


def _mla_prep_kernel(x_ref, xprev_ref, xnext_ref, ng_ref, win_ref, pool_w_ref, pool_scale_ref,
                     qg_ref, wuq_ref, kvg_ref, wuk_ref, wuvt_ref,
                     qc_ref, qs_ref, kc_ref, ks_ref,
                     a_ref, q_ref, k_ref, vt_ref, cur_ref, pad_ref, *, ts, seq):
    i = pl.program_id(1)
    n_tiles = pl.num_programs(1)

    cur_ref[0] = _dot(_rms(x_ref[0], ng_ref[...]).astype(BF16), win_ref[...])
    x_halo = jnp.concatenate([xprev_ref[0], xnext_ref[0]], axis=0)
    u_halo = _dot(_rms(x_halo, ng_ref[...]).astype(BF16), win_ref[:, 0:POOL_WIDTH])

    pad_ref[0:POOL_HALO, :] = jnp.where(i > 0, u_halo[:POOL_HALO], 0.0)
    pad_ref[POOL_HALO:POOL_HALO + ts, :] = cur_ref[0, :, 0:POOL_WIDTH]
    pad_ref[POOL_HALO + ts:2 * POOL_HALO + ts, :] = jnp.where(i < n_tiles - 1,
                                                              u_halo[POOL_HALO:], 0.0)
    t = i * ts + lax.broadcasted_iota(jnp.int32, (ts, 1), 0)
    for g, w in enumerate(POOL_WINDOWS):
        lanes = slice(g * POOL_GROUP_DIM, (g + 1) * POOL_GROUP_DIM)
        acc = None
        for off in range(-(w // 2), w // 2):
            piece = pad_ref[POOL_HALO + off:POOL_HALO + off + ts, lanes]
            acc = piece if acc is None else acc + piece
        cnt = jnp.minimum(t + w // 2, seq) - jnp.maximum(t - w // 2, 0)
        inv_cnt = 1.0 / cnt.astype(F32)
        pooled = acc * inv_cnt - cur_ref[0, :, lanes]
        a_g = _dot(pooled.astype(BF16), pool_w_ref[g]) * pool_scale_ref[:, lanes]
        a_ref[0, :, lanes] = a_g.astype(a_ref.dtype)

    c_q = cur_ref[0, :, POOL_WIDTH:POOL_WIDTH + MLA_Q_RANK]
    q = _dot(_rms(c_q, qg_ref[...]).astype(BF16), wuq_ref[...])
    qc, qs = qc_ref[...], qs_ref[...]
    partner = MLA_HEADS * HEAD_LANES
    for h in range(MLA_HEADS):
        lanes = slice(h * HEAD_LANES, (h + 1) * HEAD_LANES)
        lanes_p = slice(partner + h * HEAD_LANES, partner + (h + 1) * HEAD_LANES)
        q_ref[0, :, lanes] = (q[:, lanes] * qc + q[:, lanes_p] * qs).astype(q_ref.dtype)

    o_kv = POOL_WIDTH + MLA_Q_RANK
    c_kv = cur_ref[0, :, o_kv:o_kv + MLA_KV_RANK]
    kvn = _rms(c_kv, kvg_ref[...]).astype(BF16)
    k_nope = _dot(kvn, wuk_ref[...])
    o_kr = o_kv + MLA_KV_RANK
    k_rope = (cur_ref[0, :, o_kr:o_kr + HEAD_LANES] * kc_ref[...]
              + cur_ref[0, :, o_kr + HEAD_LANES:o_kr + 2 * HEAD_LANES] * ks_ref[...])
    for h in range(MLA_HEADS):
        lanes = slice(h * HEAD_LANES, (h + 1) * HEAD_LANES)
        k_ref[0, :, lanes] = (k_nope[:, lanes] + k_rope).astype(k_ref.dtype)
    vt_ref[0] = _dot_nt(wuvt_ref[...], kvn).astype(vt_ref.dtype)


def _mla_prep(x, norm_g, w_in, pool_w, pool_scale, qg, wuq, kvg, wuk, wuvt, q_tabs, k_tabs, ts):
    b, s, d = x.shape
    hb = ts // POOL_HALO
    n_halo = s // POOL_HALO
    tab_spec = pl.BlockSpec((ts, HEAD_LANES), lambda bi, i: (i, 0))
    kern = functools.partial(_mla_prep_kernel, ts=ts, seq=s)
    consts = [norm_g, w_in, pool_w, pool_scale, qg, wuq, kvg, wuk, wuvt]
    row_blk = lambda w: pl.BlockSpec((1, ts, w), lambda bi, i: (bi, i, 0))
    return pl.pallas_call(
        kern,
        grid=(b, s // ts),
        in_specs=[
            row_blk(d),
            pl.BlockSpec((1, POOL_HALO, d), lambda bi, i: (bi, jnp.maximum(i * hb - 1, 0), 0)),
            pl.BlockSpec((1, POOL_HALO, d),
                         lambda bi, i: (bi, jnp.minimum((i + 1) * hb, n_halo - 1), 0)),
        ] + [_const_spec(c.shape) for c in consts] + [tab_spec] * 4,
        out_specs=[
            row_blk(POOL_WIDTH),
            row_blk(MLA_HEADS * HEAD_LANES),
            row_blk(MLA_HEADS * HEAD_LANES),
            pl.BlockSpec((1, MLA_HEADS * MLA_V, ts), lambda bi, i: (bi, 0, i)),
        ],
        out_shape=[
            jax.ShapeDtypeStruct((b, s, POOL_WIDTH), BF16),
            jax.ShapeDtypeStruct((b, s, MLA_HEADS * HEAD_LANES), BF16),
            jax.ShapeDtypeStruct((b, s, MLA_HEADS * HEAD_LANES), BF16),
            jax.ShapeDtypeStruct((b, MLA_HEADS * MLA_V, s), BF16),
        ],
        scratch_shapes=[pltpu.VMEM((1, ts, PROJ_WIDTH), F32),
                        pltpu.VMEM((ts + 2 * POOL_HALO, POOL_WIDTH), F32)],
        compiler_params=_params(2),
        name="mla_prep",
    )(x, x, x, *consts, *q_tabs, *k_tabs)


def _ones_rows(cols):
    row = lax.broadcasted_iota(jnp.int32, (BF16_ROWS, cols), 0)
    return jnp.where(row == 0, 1.0, 0.0).astype(BF16)


SWEEP_ROWS = 16


def _attend_tiles(n_tiles, n_items, n_keys, scores, values, finish, s_ref, e_ref, extra=None):
    n_chunks = n_keys // KEY_CHUNK
    tq = s_ref.shape[-1]
    assert n_items % 2 == 0 and n_items >= 2 and n_chunks % 2 == 0

    def park(t, i, c):
        s_ref[i % 2, c * KEY_CHUNK:(c + 1) * KEY_CHUNK, :] = scores(t, i, c)

    def logits(i, c, r):
        blk = s_ref[i % 2, c * KEY_CHUNK + r:c * KEY_CHUNK + r + SWEEP_ROWS, :]
        add = None if extra is None else extra(i % n_items, c, r)
        return blk if add is None else blk + add

    def block_max(i, c, r, m8):
        blk = logits(i, c, r).reshape(SWEEP_ROWS // SUBLANES, SUBLANES, tq)
        for part in range(SWEEP_ROWS // SUBLANES):
            m8 = blk[part] if m8 is None else jnp.maximum(m8, blk[part])
        return m8

    m8 = None
    for c in range(n_chunks):
        park(0, 0, c)
    for c in range(n_chunks):
        for r in range(0, KEY_CHUNK, SWEEP_ROWS):
            m8 = block_max(0, c, r, m8)
    park(0, 1, 0)

    def tile(t, m8):
        t_next = jnp.minimum(t + 1, n_tiles - 1)
        accs = [None] * n_items

        def value_matmul(pos):
            i, c = divmod(pos, n_chunks)
            pv = _dot(values(t, i, c), e_ref[pos % 2])
            accs[i] = pv if accs[i] is None else accs[i] + pv

        for i in range(n_items):
            m = jnp.broadcast_to(jnp.max(m8, axis=0, keepdims=True), (SWEEP_ROWS, tq))
            m8 = None
            for c in range(n_chunks):
                pos = i * n_chunks + c
                ahead_i, ahead_c = divmod(pos + n_chunks + 1, n_chunks)
                if ahead_i < n_items:
                    park(t, ahead_i, ahead_c)
                else:
                    park(t_next, ahead_i - n_items, ahead_c)
                for r in range(0, KEY_CHUNK, SWEEP_ROWS):
                    e_ref[pos % 2, r:r + SWEEP_ROWS, :] = (
                        jnp.exp2(logits(i, c, r) - m).astype(BF16))
                    m8 = block_max(i + 1, c, r, m8)
                if pos > 0:
                    value_matmul(pos - 1)
        value_matmul(n_items * n_chunks - 1)
        finish(t, accs)
        return m8

    lax.fori_loop(0, n_tiles, tile, m8)


def _mla_attn_kernel(q_ref, k_ref, vt_ref, o_ref, s_ref, e_ref):
    tq = s_ref.shape[-1]
    ones = _ones_rows(KEY_CHUNK)

    def tile_rows(t):
        return pl.ds(pl.multiple_of(t * tq, tq), tq)

    q_t = {}

    def scores(t, j, c):
        lanes = slice(j * HEAD_LANES, (j + 1) * HEAD_LANES)
        keys = slice(c * KEY_CHUNK, (c + 1) * KEY_CHUNK)
        if (id(t), j) not in q_t:
            q_t[id(t), j] = q_ref[0, tile_rows(t), lanes].T
        return _dot(k_ref[0, keys, lanes], q_t[id(t), j])

    def values(t, j, c):
        vt = vt_ref[0, j * MLA_V:(j + 1) * MLA_V, c * KEY_CHUNK:(c + 1) * KEY_CHUNK]
        return jnp.concatenate([vt, ones], axis=0)

    def finish(t, accs):
        outs = [a[:MLA_V] * (1.0 / a[MLA_V:MLA_V + 1]) for a in accs]
        o_ref[0, tile_rows(t), :] = jnp.concatenate(outs, axis=0).T.astype(o_ref.dtype)

    _attend_tiles(q_ref.shape[1] // tq, MLA_GROUP, k_ref.shape[1], scores, values, finish,
                  s_ref, e_ref)


def _mla_attn(q, k, vt):
    b, s, _ = q.shape
    tq, g = ATTN_TQ, MLA_GROUP
    return pl.pallas_call(
        _mla_attn_kernel,
        grid=(b, MLA_HEADS // g),
        in_specs=[
            pl.BlockSpec((1, s, g * HEAD_LANES), lambda bi, p: (bi, 0, p)),
            pl.BlockSpec((1, s, g * HEAD_LANES), lambda bi, p: (bi, 0, p)),
            pl.BlockSpec((1, g * MLA_V, s), lambda bi, p: (bi, p, 0)),
        ],
        out_specs=pl.BlockSpec((1, s, g * MLA_V), lambda bi, p: (bi, 0, p)),
        out_shape=jax.ShapeDtypeStruct((b, s, MLA_HEADS * MLA_V), BF16),
        scratch_shapes=[pltpu.VMEM((2, s, tq), F32), pltpu.VMEM((2, KEY_CHUNK, tq), BF16)],
        compiler_params=_params(2),
        name="mla_attn",
    )(q, k, vt)


def _bf16_pieces(x):
    p1 = x.astype(BF16).astype(F32)
    p2 = (x - p1).astype(BF16).astype(F32)
    p3 = (x - p1 - p2).astype(BF16).astype(F32)
    return p1, p2, p3


def _piece_lanes(lane, pieces):
    sel = lane % 3
    return jnp.where(sel == 0, pieces[0], jnp.where(sel == 1, pieces[1], pieces[2]))


def _diff_attn_kernel(q_ref, k_ref, vt_ref, slope_ref, lq1_ref, lk1_ref, lq2_ref, lk2_ref,
                      g_ref, o_ref, kaug_ref, dbias_ref, s_ref, e_ref, *, lam_init):
    bi = pl.program_id(1)
    tq = s_ref.shape[-1]
    n_keys = k_ref.shape[1]
    n_chunks = n_keys // KEY_CHUNK
    n_tiles = q_ref.shape[1] // tq
    half = KEY_CHUNK // 2
    aug = HEAD_LANES

    slopes = [slope_ref[g, :, 0:1] for g in range(DIFF_GROUP)]
    pieces = [_bf16_pieces(c) for c in slopes]

    @pl.when(bi == 0)
    def _():
        j = lax.broadcasted_iota(jnp.int32, (n_keys, aug), 0)
        lane = lax.broadcasted_iota(jnp.int32, (n_keys, aug), 1)
        chunk = j // KEY_CHUNK
        dj = j % KEY_CHUNK
        lo = (dj % half).astype(F32)
        dk = lax.broadcasted_iota(jnp.int32, (KEY_CHUNK, tq), 0)
        dq = lax.broadcasted_iota(jnp.int32, (KEY_CHUNK, tq), 1)
        dist = jnp.abs(dq - dk).astype(F32)
        for g in range(DIFF_GROUP):
            neg_c = -_piece_lanes(lane, pieces[g])
            dbias_ref[g] = -slopes[g] * dist
            for qi in range(n_tiles):
                side = jnp.sign(qi - chunk).astype(F32)
                hi = (dj - dj % half - (qi - chunk) * KEY_CHUNK).astype(F32)
                val = jnp.where(lane < 6, neg_c,
                                jnp.where(lane < 9, lo, jnp.where(lane < 12, hi, 0.0)))
                kaug_ref[qi, g, :, aug:] = (side * val).astype(BF16)

    lam = (jnp.exp(jnp.sum(lq1_ref[...] * lk1_ref[...], axis=-1, keepdims=True))
           - jnp.exp(jnp.sum(lq2_ref[...] * lk2_ref[...], axis=-1, keepdims=True))
           + lam_init)

    lane = lax.broadcasted_iota(jnp.int32, (tq, HEAD_LANES), 1)
    di = lax.broadcasted_iota(jnp.int32, (tq, aug), 0)
    q_lo = (di % half).astype(F32)
    q_hi = (di - di % half).astype(F32)
    q_augs = []
    for g in range(DIFF_GROUP):
        k = k_ref[0, :, g * HEAD_LANES:(g + 1) * HEAD_LANES]
        for qi in range(n_tiles):
            kaug_ref[qi, g, :, :aug] = k
        c_lanes = _piece_lanes(lane, pieces[g])
        q_augs.append(jnp.where(lane < 3, q_lo, jnp.where(lane < 6, q_hi,
                                                           jnp.where(lane < 12, c_lanes, 0.0))))
    ones = _ones_rows(KEY_CHUNK)
    q_maps = {}

    def tile_rows(t):
        return pl.ds(pl.multiple_of(t * tq, tq), tq)

    def chunk_of(t, c):
        return (t + c) % n_chunks

    def scores(t, i, c):
        g, m = divmod(i, 2)
        if (id(t), i) not in q_maps:
            q = q_ref[0, tile_rows(t), g * HEAD_LANES:(g + 1) * HEAD_LANES]
            in_map = (lane >= m * DIFF_HEAD_DIM) & (lane < (m + 1) * DIFF_HEAD_DIM)
            q_m = jnp.where(in_map, q, jnp.zeros_like(q))
            q_cat = jnp.concatenate([q_m.astype(F32), q_augs[g]], axis=1)
            q_maps[id(t), i] = q_cat.T.astype(BF16)
        rows = pl.ds(pl.multiple_of(chunk_of(t, c) * KEY_CHUNK, KEY_CHUNK), KEY_CHUNK)
        return _dot(kaug_ref[t, g, rows, :], q_maps[id(t), i])

    def values(t, i, c):
        g = i // 2
        vt = vt_ref[0, chunk_of(t, c), g * HEAD_LANES:(g + 1) * HEAD_LANES, :]
        return jnp.concatenate([vt, ones], axis=0)

    def diagonal_bias(i, c, r):
        return dbias_ref[i // 2, r:r + SWEEP_ROWS, :] if c == 0 else None

    def finish(t, accs):
        d = HEAD_LANES
        for g in range(DIFF_GROUP):
            a0, a1 = accs[2 * g], accs[2 * g + 1]
            o_t = a0[:d] * (1.0 / a0[d:d + 1]) - a1[:d] * (lam / a1[d:d + 1])
            o = _rms(o_t.T, g_ref[...]) * (1.0 - lam_init)
            o_ref[0, tile_rows(t), g * d:(g + 1) * d] = o.astype(o_ref.dtype)

    _attend_tiles(n_tiles, 2 * DIFF_GROUP, n_keys, scores, values, finish, s_ref, e_ref,
                  diagonal_bias)


def _diff_attn(qk, vt, slopes, lq1, lk1, lq2, lk2, subln_g, lam_init):
    b, s, _ = qk.shape
    tq, g = ATTN_TQ, DIFF_GROUP
    n_groups = DIFF_HEADS // g
    n_chunks = s // KEY_CHUNK
    assert tq == KEY_CHUNK and vt.shape == (b, n_chunks, DIFF_HEADS * HEAD_LANES, KEY_CHUNK)
    kern = functools.partial(_diff_attn_kernel, lam_init=lam_init)
    vec = lambda a: _const_spec(a.shape)
    return pl.pallas_call(
        kern,
        grid=(n_groups, b),
        in_specs=[
            pl.BlockSpec((1, s, g * HEAD_LANES), lambda hi, bi: (bi, 0, hi)),
            pl.BlockSpec((1, s, g * HEAD_LANES), lambda hi, bi: (bi, 0, n_groups + hi)),
            pl.BlockSpec((1, n_chunks, g * HEAD_LANES, KEY_CHUNK), lambda hi, bi: (bi, 0, hi, 0)),
            pl.BlockSpec((g, 1, HEAD_LANES), lambda hi, bi: (hi, 0, 0)),
            vec(lq1), vec(lk1), vec(lq2), vec(lk2), vec(subln_g),
        ],
        out_specs=pl.BlockSpec((1, s, g * HEAD_LANES), lambda hi, bi: (bi, 0, hi)),
        out_shape=jax.ShapeDtypeStruct((b, s, DIFF_HEADS * HEAD_LANES), BF16),
        scratch_shapes=[pltpu.VMEM((s // tq, g, s, 2 * HEAD_LANES), BF16),
                        pltpu.VMEM((g, KEY_CHUNK, tq), F32),
                        pltpu.VMEM((2, s, tq), F32),
                        pltpu.VMEM((2, KEY_CHUNK, tq), BF16)],
        compiler_params=_params(2),
        name="diff_attn",
    )(qk, qk, vt, slopes, lq1, lk1, lq2, lk2, subln_g)


def _post_kernel(*refs, n_mix, final_norm):
    h_ref = refs[0]
    mix_refs = refs[1:1 + n_mix]
    wo_ref, ng_ref, wg_ref, wu_ref, wd_ref = refs[1 + n_mix:6 + n_mix]
    rest = refs[6 + n_mix:]
    fg_ref = rest[0] if final_norm else None
    o_ref = rest[-1]

    h = h_ref[...]
    row0 = 0
    for m_ref in mix_refs:
        width = m_ref.shape[1]
        h = h + _dot(m_ref[...], wo_ref[row0:row0 + width, :])
        row0 += width
    hn = _rms(h, ng_ref[...]).astype(BF16)
    gate = _dot(hn, wg_ref[...])
    up = _dot(hn, wu_ref[...])
    act = gate * jax.nn.sigmoid(gate) * up
    h = h + _dot(act.astype(BF16), wd_ref[...])
    if final_norm:
        h = _rms(h, fg_ref[...])
    o_ref[...] = h


def _post(h, mixes, w_out, norm_g, ffn_weights, layer, final_g, tm):
    n, d = h.shape
    n_mix = len(mixes)
    final_norm = final_g is not None
    kern = functools.partial(_post_kernel, n_mix=n_mix, final_norm=final_norm)
    row_spec = lambda a: pl.BlockSpec((tm, a.shape[1]), lambda i: (i, 0))
    layer_spec = lambda a: pl.BlockSpec((None,) + a.shape[1:], lambda i: (layer, 0, 0),
                                        pipeline_mode=pl.Buffered(1))
    consts = [w_out, norm_g] + list(ffn_weights) + ([final_g] if final_norm else [])
    const_specs = ([_const_spec(w_out.shape), _const_spec(norm_g.shape)]
                   + [layer_spec(w) for w in ffn_weights]
                   + ([_const_spec(final_g.shape)] if final_norm else []))
    return pl.pallas_call(
        kern,
        grid=(n // tm,),
        in_specs=[row_spec(h)] + [row_spec(m) for m in mixes] + const_specs,
        out_specs=pl.BlockSpec((tm, d), lambda i: (i, 0)),
        out_shape=jax.ShapeDtypeStruct((n, d), F32),
        compiler_params=_params(1),
        name="post_final" if final_norm else "post",
    )(h, *mixes, *consts)


def _rope_head_blocks(w_rope, lead):
    half = MLA_ROPE // 2
    tail = HEAD_LANES - lead - MLA_ROPE
    widths = [(0, 0)] * (w_rope.ndim - 1) + [(lead, tail)]
    swapped = jnp.concatenate([w_rope[..., half:], w_rope[..., :half]], axis=-1)
    return jnp.pad(w_rope, widths), jnp.pad(swapped, widths)


def _hybrid_weights(w_in, w_uq, w_ukv):
    o1 = POOL_WIDTH + MLA_Q_RANK + MLA_KV_RANK
    k_r, k_r_sw = _rope_head_blocks(w_in[:, o1:], MLA_NOPE)
    w_in_p = jnp.concatenate([w_in[:, :o1], k_r, k_r_sw], axis=1).astype(BF16)
    r = w_uq.shape[0]
    wq = w_uq.reshape(r, MLA_HEADS, MLA_NOPE + MLA_ROPE)
    q_r, q_r_sw = _rope_head_blocks(wq[:, :, MLA_NOPE:], MLA_NOPE)
    nope = jnp.pad(wq[:, :, :MLA_NOPE], ((0, 0), (0, 0), (0, HEAD_LANES - MLA_NOPE)))
    wq = jnp.concatenate([(nope + q_r).reshape(r, -1), q_r_sw.reshape(r, -1)],
                         axis=1).astype(BF16)
    wkv = w_ukv.reshape(r, MLA_HEADS, MLA_NOPE + MLA_V)
    wk = jnp.pad(wkv[:, :, :MLA_NOPE], ((0, 0), (0, 0), (0, HEAD_LANES - MLA_NOPE)))
    wk = wk.reshape(r, MLA_HEADS * HEAD_LANES).astype(BF16)
    wvt = wkv[:, :, MLA_NOPE:].reshape(r, MLA_HEADS * MLA_V).T.astype(BF16)
    return w_in_p, wq, wk, wvt


def _rope_tables(seq, scale):
    half = MLA_ROPE // 2
    inv = ROPE_THETA ** (-jnp.arange(0, MLA_ROPE, 2, dtype=F32) / MLA_ROPE)
    ang = jnp.arange(seq, dtype=F32)[:, None] * inv[None, :]
    cos, sin = jnp.cos(ang), jnp.sin(ang)
    z = lambda w: jnp.zeros((seq, w), F32)
    tail = HEAD_LANES - MLA_NOPE - MLA_ROPE
    c = jnp.concatenate([jnp.ones((seq, MLA_NOPE), F32), cos, cos, z(tail)], axis=1)
    sn = jnp.concatenate([z(MLA_NOPE), -sin, sin, z(tail)], axis=1)
    return c * scale, sn * scale


def kernel(x, attn_norm_g, ffn_norm_g, final_norm_g, hyb_w_in, pool_w, pool_scale,
           mla_q_norm_g, mla_w_uq, mla_kv_norm_g, mla_w_ukv, hyb_w_out,
           diff_w_qkv, diff_lambda_q1, diff_lambda_k1, diff_lambda_q2, diff_lambda_k2,
           diff_subln_g, diff_w_out, ffn_w_gate, ffn_w_up, ffn_w_down):
    b, s, d = x.shape
    n = b * s
    row = lambda a: a.reshape(1, -1)
    h = x.reshape(n, d)

    w_in_p, wq_p, wk_p, wvt_p = _hybrid_weights(hyb_w_in[0], mla_w_uq[0], mla_w_ukv[0])
    q_tabs = _rope_tables(s, float((MLA_NOPE + MLA_ROPE) ** -0.5 * LOG2E))
    k_tabs = _rope_tables(s, 1.0)
    a_out, q, k, vt = _mla_prep(
        x, row(attn_norm_g[0]), w_in_p, pool_w[0].astype(BF16), row(pool_scale[0]),
        row(mla_q_norm_g[0]), wq_p, row(mla_kv_norm_g[0]), wk_p, wvt_p, q_tabs, k_tabs, 1024)
    b_out = _mla_attn(q, k, vt)
    ffn_weights = (ffn_w_gate.astype(BF16), ffn_w_up.astype(BF16), ffn_w_down.astype(BF16))
    h = _post(h, [a_out.reshape(n, -1), b_out.reshape(n, -1)], hyb_w_out[0].astype(BF16),
              row(ffn_norm_g[0]), ffn_weights, 0, None, 512)

    qk_w = DIFF_HEADS * 2 * DIFF_HEAD_DIM
    w_qkv = diff_w_qkv[0]
    col_scale = jnp.where(jnp.arange(2 * qk_w) < qk_w, DIFF_HEAD_DIM ** -0.5 * LOG2E, 1.0)
    w_qk = (w_qkv[:, :2 * qk_w] * col_scale.astype(F32)).astype(BF16)
    w_vt = w_qkv[:, 2 * qk_w:].astype(BF16).T
    qk, vt = _norm_qkv(h.reshape(b, s, d), row(attn_norm_g[1]), w_qk, w_vt, KEY_CHUNK)
    slopes = 2.0 ** (-8.0 * jnp.arange(1, DIFF_HEADS + 1, dtype=F32) / DIFF_HEADS)
    slopes2 = jnp.broadcast_to((LOG2E * slopes)[:, None, None], (DIFF_HEADS, 1, HEAD_LANES))
    lam_init = 0.8 - 0.6 * math.exp(-0.3 * 1)
    o = _diff_attn(qk, vt, slopes2, row(diff_lambda_q1[0]),
                   row(diff_lambda_k1[0]), row(diff_lambda_q2[0]), row(diff_lambda_k2[0]),
                   row(diff_subln_g[0]), lam_init)
    h = _post(h, [o.reshape(n, -1)], diff_w_out[0].astype(BF16),
              row(ffn_norm_g[1]), ffn_weights, 1, row(final_norm_g), 512)
    return h.reshape(b, s, d)
```

```python
import functools
import math

import jax
import jax.numpy as jnp
from jax import lax
from jax.experimental import pallas as pl
from jax.experimental.pallas import tpu as pltpu

D_MODEL = 1024
EPS = 1e-6
LOG2E = math.log2(math.e)

POOL_WINDOWS = (2, 4, 8, 16)
POOL_GROUP_DIM = 128
POOL_WIDTH = 512
POOL_HALO = 8

MLA_HEADS = 8
MLA_NOPE = 64
MLA_ROPE = 32
MLA_V = 64
MLA_Q_RANK = 256
MLA_KV_RANK = 256
ROPE_THETA = 10000.0
HEAD_LANES = 128
PROJ_WIDTH = POOL_WIDTH + MLA_Q_RANK + MLA_KV_RANK + 2 * HEAD_LANES

DIFF_HEADS = 8
DIFF_HEAD_DIM = 64
D_FF = 2816

SUBLANES = 8
BF16_ROWS = 16
KEY_CHUNK = 512
ATTN_TQ = 512
MLA_GROUP = 8
DIFF_GROUP = 4

VMEM_LIMIT = 56 * 1024 * 1024

F32 = jnp.float32
BF16 = jnp.bfloat16


def _rms(x, g):
    return x * lax.rsqrt(jnp.mean(x * x, axis=-1, keepdims=True) + EPS) * g


def _dot(a, b):
    return jnp.dot(a, b, preferred_element_type=F32)


def _dot_nt(a, b):
    return lax.dot_general(a, b, (((1,), (1,)), ((), ())), preferred_element_type=F32)


def _const_spec(shape):
    nd = len(shape)
    return pl.BlockSpec(shape, lambda *_: (0,) * nd, pipeline_mode=pl.Buffered(1))


def _params(n_axes):
    return pltpu.CompilerParams(dimension_semantics=("arbitrary",) * n_axes,
                                vmem_limit_bytes=VMEM_LIMIT)


def _norm_qkv_kernel(x_ref, g_ref, wqk_ref, wvt_ref, qk_ref, vt_ref):
    y = _rms(x_ref[0], g_ref[...]).astype(BF16)
    qk_ref[0] = _dot(y, wqk_ref[...]).astype(qk_ref.dtype)
    vt_ref[0, 0] = _dot_nt(wvt_ref[...], y).astype(vt_ref.dtype)


def _norm_qkv(x, g, w_qk, w_vt, tm):
    b, s, d = x.shape
    return pl.pallas_call(
        _norm_qkv_kernel,
        grid=(b, s // tm),
        in_specs=[pl.BlockSpec((1, tm, d), lambda bi, i: (bi, i, 0)),
                  _const_spec(g.shape), _const_spec(w_qk.shape), _const_spec(w_vt.shape)],
        out_specs=[pl.BlockSpec((1, tm, w_qk.shape[1]), lambda bi, i: (bi, i, 0)),
                   pl.BlockSpec((1, 1, w_vt.shape[0], tm), lambda bi, i: (bi, i, 0, 0))],
        out_shape=[jax.ShapeDtypeStruct((b, s, w_qk.shape[1]), BF16),
                   jax.ShapeDtypeStruct((b, s // tm, w_vt.shape[0], tm), BF16)],
        compiler_params=_params(2),
        name="norm_qkv",
    )(x, g, w_qk, w_vt)


def _mla_prep_kernel(x_ref, xprev_ref, xnext_ref, ng_ref, win_ref, pool_w_ref, pool_scale_ref,
                     qg_ref, wuq_ref, kvg_ref, wuk_ref, wuvt_ref,
                     qc_ref, qs_ref, kc_ref, ks_ref,
                     a_ref, q_ref, k_ref, vt_ref, cur_ref, pad_ref, *, ts, seq):
    i = pl.program_id(1)
    n_tiles = pl.num_programs(1)

    cur_ref[0] = _dot(_rms(x_ref[0], ng_ref[...]).astype(BF16), win_ref[...])
    x_halo = jnp.concatenate([xprev_ref[0], xnext_ref[0]], axis=0)
    u_halo = _dot(_rms(x_halo, ng_ref[...]).astype(BF16), win_ref[:, 0:POOL_WIDTH])

    pad_ref[0:POOL_HALO, :] = jnp.where(i > 0, u_halo[:POOL_HALO], 0.0)
    pad_ref[POOL_HALO:POOL_HALO + ts, :] = cur_ref[0, :, 0:POOL_WIDTH]
    pad_ref[POOL_HALO + ts:2 * POOL_HALO + ts, :] = jnp.where(i < n_tiles - 1,
                                                              u_halo[POOL_HALO:], 0.0)
    t = i * ts + lax.broadcasted_iota(jnp.int32, (ts, 1), 0)
    for g, w in enumerate(POOL_WINDOWS):
        lanes = slice(g * POOL_GROUP_DIM, (g + 1) * POOL_GROUP_DIM)
        acc = None
        for off in range(-(w // 2), w // 2):
            piece = pad_ref[POOL_HALO + off:POOL_HALO + off + ts, lanes]
            acc = piece if acc is None else acc + piece
        cnt = jnp.minimum(t + w // 2, seq) - jnp.maximum(t - w // 2, 0)
        inv_cnt = 1.0 / cnt.astype(F32)
        pooled = acc * inv_cnt - cur_ref[0, :, lanes]
        a_g = _dot(pooled.astype(BF16), pool_w_ref[g]) * pool_scale_ref[:, lanes]
        a_ref[0, :, lanes] = a_g.astype(a_ref.dtype)

    c_q = cur_ref[0, :, POOL_WIDTH:POOL_WIDTH + MLA_Q_RANK]
    q = _dot(_rms(c_q, qg_ref[...]).astype(BF16), wuq_ref[...])
    qc, qs = qc_ref[...], qs_ref[...]
    partner = MLA_HEADS * HEAD_LANES
    for h in range(MLA_HEADS):
        lanes = slice(h * HEAD_LANES, (h + 1) * HEAD_LANES)
        lanes_p = slice(partner + h * HEAD_LANES, partner + (h + 1) * HEAD_LANES)
        q_ref[0, :, lanes] = (q[:, lanes] * qc + q[:, lanes_p] * qs).astype(q_ref.dtype)

    o_kv = POOL_WIDTH + MLA_Q_RANK
    c_kv = cur_ref[0, :, o_kv:o_kv + MLA_KV_RANK]
    kvn = _rms(c_kv, kvg_ref[...]).astype(BF16)
    k_nope = _dot(kvn, wuk_ref[...])
    o_kr = o_kv + MLA_KV_RANK
    k_rope = (cur_ref[0, :, o_kr:o_kr + HEAD_LANES] * kc_ref[...]
              + cur_ref[0, :, o_kr + HEAD_LANES:o_kr + 2 * HEAD_LANES] * ks_ref[...])
    for h in range(MLA_HEADS):
        lanes = slice(h * HEAD_LANES, (h + 1) * HEAD_LANES)
        k_ref[0, :, lanes] = (k_nope[:, lanes] + k_rope).astype(k_ref.dtype)
    vt_ref[0] = _dot_nt(wuvt_ref[...], kvn).astype(vt_ref.dtype)


def _mla_prep(x, norm_g, w_in, pool_w, pool_scale, qg, wuq, kvg, wuk, wuvt, q_tabs, k_tabs, ts):
    b, s, d = x.shape
    hb = ts // POOL_HALO
    n_halo = s // POOL_HALO
    tab_spec = pl.BlockSpec((ts, HEAD_LANES), lambda bi, i: (i, 0))
    kern = functools.partial(_mla_prep_kernel, ts=ts, seq=s)
    consts = [norm_g, w_in, pool_w, pool_scale, qg, wuq, kvg, wuk, wuvt]
    row_blk = lambda w: pl.BlockSpec((1, ts, w), lambda bi, i: (bi, i, 0))
    return pl.pallas_call(
        kern,
        grid=(b, s // ts),
        in_specs=[
            row_blk(d),
            pl.BlockSpec((1, POOL_HALO, d), lambda bi, i: (bi, jnp.maximum(i * hb - 1, 0), 0)),
            pl.BlockSpec((1, POOL_HALO, d),
                         lambda bi, i: (bi, jnp.minimum((i + 1) * hb, n_halo - 1), 0)),
        ] + [_const_spec(c.shape) for c in consts] + [tab_spec] * 4,
        out_specs=[
            row_blk(POOL_WIDTH),
            row_blk(MLA_HEADS * HEAD_LANES),
            row_blk(MLA_HEADS * HEAD_LANES),
            pl.BlockSpec((1, MLA_HEADS * MLA_V, ts), lambda bi, i: (bi, 0, i)),
        ],
        out_shape=[
            jax.ShapeDtypeStruct((b, s, POOL_WIDTH), BF16),
            jax.ShapeDtypeStruct((b, s, MLA_HEADS * HEAD_LANES), BF16),
            jax.ShapeDtypeStruct((b, s, MLA_HEADS * HEAD_LANES), BF16),
            jax.ShapeDtypeStruct((b, MLA_HEADS * MLA_V, s), BF16),
        ],
        scratch_shapes=[pltpu.VMEM((1, ts, PROJ_WIDTH), F32),
                        pltpu.VMEM((ts + 2 * POOL_HALO, POOL_WIDTH), F32)],
        compiler_params=_params(2),
        name="mla_prep",
    )(x, x, x, *consts, *q_tabs, *k_tabs)


def _ones_rows(cols):
    row = lax.broadcasted_iota(jnp.int32, (BF16_ROWS, cols), 0)
    return jnp.where(row == 0, 1.0, 0.0).astype(BF16)


SWEEP_ROWS = 16


def _attend_tiles(n_tiles, n_items, n_keys, scores, values, finish, s_ref, e_ref, extra=None):
    n_chunks = n_keys // KEY_CHUNK
    tq = s_ref.shape[-1]
    assert n_items % 2 == 0 and n_items >= 2 and n_chunks % 2 == 0

    def park(t, i, c):
        s_ref[i % 2, c * KEY_CHUNK:(c + 1) * KEY_CHUNK, :] = scores(t, i, c)

    def logits(i, c, r):
        blk = s_ref[i % 2, c * KEY_CHUNK + r:c * KEY_CHUNK + r + SWEEP_ROWS, :]
        add = None if extra is None else extra(i % n_items, c, r)
        return blk if add is None else blk + add

    def block_max(i, c, r, m8):
        blk = logits(i, c, r).reshape(SWEEP_ROWS // SUBLANES, SUBLANES, tq)
        for part in range(SWEEP_ROWS // SUBLANES):
            m8 = blk[part] if m8 is None else jnp.maximum(m8, blk[part])
        return m8

    m8 = None
    for c in range(n_chunks):
        park(0, 0, c)
    for c in range(n_chunks):
        for r in range(0, KEY_CHUNK, SWEEP_ROWS):
            m8 = block_max(0, c, r, m8)
    park(0, 1, 0)

    def tile(t, m8):
        t_next = jnp.minimum(t + 1, n_tiles - 1)
        accs = [None] * n_items

        def value_matmul(pos):
            i, c = divmod(pos, n_chunks)
            pv = _dot(values(t, i, c), e_ref[pos % 2])
            accs[i] = pv if accs[i] is None else accs[i] + pv

        for i in range(n_items):
            m = jnp.broadcast_to(jnp.max(m8, axis=0, keepdims=True), (SWEEP_ROWS, tq))
            m8 = None
            for c in range(n_chunks):
                pos = i * n_chunks + c
                ahead_i, ahead_c = divmod(pos + n_chunks + 1, n_chunks)
                if ahead_i < n_items:
                    park(t, ahead_i, ahead_c)
                else:
                    park(t_next, ahead_i - n_items, ahead_c)
                for r in range(0, KEY_CHUNK, SWEEP_ROWS):
                    e_ref[pos % 2, r:r + SWEEP_ROWS, :] = (
                        jnp.exp2(logits(i, c, r) - m).astype(BF16))
                    m8 = block_max(i + 1, c, r, m8)
                if pos > 0:
                    value_matmul(pos - 1)
        value_matmul(n_items * n_chunks - 1)
        finish(t, accs)
        return m8

    lax.fori_loop(0, n_tiles, tile, m8)


def _mla_attn_kernel(q_ref, k_ref, vt_ref, o_ref, kc_ref, vc_ref, s_ref, e_ref):
    tq = s_ref.shape[-1]
    n_keys = k_ref.shape[1]

    for j in range(MLA_GROUP):
        kc_ref[j] = k_ref[0, :, j * HEAD_LANES:(j + 1) * HEAD_LANES]
        vc_ref[j, :MLA_V, :] = vt_ref[0, j * MLA_V:(j + 1) * MLA_V, :]
        vc_ref[j, MLA_V:, :] = _ones_rows(n_keys)

    def tile_rows(t):
        return pl.ds(pl.multiple_of(t * tq, tq), tq)

    q_t = {}

    def scores(t, j, c):
        lanes = slice(j * HEAD_LANES, (j + 1) * HEAD_LANES)
        if (id(t), j) not in q_t:
            q_t[id(t), j] = q_ref[0, tile_rows(t), lanes].T
        return _dot(kc_ref[j, c * KEY_CHUNK:(c + 1) * KEY_CHUNK, :], q_t[id(t), j])

    def values(t, j, c):
        return vc_ref[j, :, c * KEY_CHUNK:(c + 1) * KEY_CHUNK]

    def finish(t, accs):
        outs = [a[:MLA_V] * (1.0 / a[MLA_V:MLA_V + 1]) for a in accs]
        o_ref[0, tile_rows(t), :] = jnp.concatenate(outs, axis=0).T.astype(o_ref.dtype)

    _attend_tiles(q_ref.shape[1] // tq, MLA_GROUP, k_ref.shape[1], scores, values, finish,
                  s_ref, e_ref)


def _mla_attn(q, k, vt):
    b, s, _ = q.shape
    tq, g = ATTN_TQ, MLA_GROUP
    return pl.pallas_call(
        _mla_attn_kernel,
        grid=(b, MLA_HEADS // g),
        in_specs=[
            pl.BlockSpec((1, s, g * HEAD_LANES), lambda bi, p: (bi, 0, p)),
            pl.BlockSpec((1, s, g * HEAD_LANES), lambda bi, p: (bi, 0, p)),
            pl.BlockSpec((1, g * MLA_V, s), lambda bi, p: (bi, p, 0)),
        ],
        out_specs=pl.BlockSpec((1, s, g * MLA_V), lambda bi, p: (bi, 0, p)),
        out_shape=jax.ShapeDtypeStruct((b, s, MLA_HEADS * MLA_V), BF16),
        scratch_shapes=[pltpu.VMEM((g, s, HEAD_LANES), BF16),
                        pltpu.VMEM((g, MLA_V + BF16_ROWS, s), BF16),
                        pltpu.VMEM((2, s, tq), F32),
                        pltpu.VMEM((2, KEY_CHUNK, tq), BF16)],
        compiler_params=_params(2),
        name="mla_attn",
    )(q, k, vt)


def _bf16_pieces(x):
    p1 = x.astype(BF16).astype(F32)
    p2 = (x - p1).astype(BF16).astype(F32)
    p3 = (x - p1 - p2).astype(BF16).astype(F32)
    return p1, p2, p3


def _piece_lanes(lane, pieces):
    sel = lane % 3
    return jnp.where(sel == 0, pieces[0], jnp.where(sel == 1, pieces[1], pieces[2]))


def _diff_attn_kernel(q_ref, k_ref, vt_ref, slope_ref, lq1_ref, lk1_ref, lq2_ref, lk2_ref,
                      g_ref, o_ref, kaug_ref, dbias_ref, s_ref, e_ref, *, lam_init):
    bi = pl.program_id(1)
    tq = s_ref.shape[-1]
    n_keys = k_ref.shape[1]
    n_chunks = n_keys // KEY_CHUNK
    n_tiles = q_ref.shape[1] // tq
    half = KEY_CHUNK // 2
    aug = HEAD_LANES

    slopes = [slope_ref[g, :, 0:1] for g in range(DIFF_GROUP)]
    pieces = [_bf16_pieces(c) for c in slopes]

    @pl.when(bi == 0)
    def _():
        j = lax.broadcasted_iota(jnp.int32, (n_keys, aug), 0)
        lane = lax.broadcasted_iota(jnp.int32, (n_keys, aug), 1)
        chunk = j // KEY_CHUNK
        dj = j % KEY_CHUNK
        lo = (dj % half).astype(F32)
        dk = lax.broadcasted_iota(jnp.int32, (KEY_CHUNK, tq), 0)
        dq = lax.broadcasted_iota(jnp.int32, (KEY_CHUNK, tq), 1)
        dist = jnp.abs(dq - dk).astype(F32)
        for g in range(DIFF_GROUP):
            neg_c = -_piece_lanes(lane, pieces[g])
            dbias_ref[g] = -slopes[g] * dist
            for qi in range(n_tiles):
                side = jnp.sign(qi - chunk).astype(F32)
                hi = (dj - dj % half - (qi - chunk) * KEY_CHUNK).astype(F32)
                val = jnp.where(lane < 6, neg_c,
                                jnp.where(lane < 9, lo, jnp.where(lane < 12, hi, 0.0)))
                kaug_ref[qi, g, :, aug:] = (side * val).astype(BF16)

    lam = (jnp.exp(jnp.sum(lq1_ref[...] * lk1_ref[...], axis=-1, keepdims=True))
           - jnp.exp(jnp.sum(lq2_ref[...] * lk2_ref[...], axis=-1, keepdims=True))
           + lam_init)

    lane = lax.broadcasted_iota(jnp.int32, (tq, HEAD_LANES), 1)
    di = lax.broadcasted_iota(jnp.int32, (tq, aug), 0)
    q_lo = (di % half).astype(F32)
    q_hi = (di - di % half).astype(F32)
    q_augs = []
    for g in range(DIFF_GROUP):
        k = k_ref[0, :, g * HEAD_LANES:(g + 1) * HEAD_LANES]
        for qi in range(n_tiles):
            kaug_ref[qi, g, :, :aug] = k
        c_lanes = _piece_lanes(lane, pieces[g])
        q_augs.append(jnp.where(lane < 3, q_lo, jnp.where(lane < 6, q_hi,
                                                           jnp.where(lane < 12, c_lanes, 0.0))))
    ones = _ones_rows(KEY_CHUNK)
    q_maps = {}

    def tile_rows(t):
        return pl.ds(pl.multiple_of(t * tq, tq), tq)

    def chunk_of(t, c):
        return (t + c) % n_chunks

    def scores(t, i, c):
        g, m = divmod(i, 2)
        if (id(t), i) not in q_maps:
            q = q_ref[0, tile_rows(t), g * HEAD_LANES:(g + 1) * HEAD_LANES]
            in_map = (lane >= m * DIFF_HEAD_DIM) & (lane < (m + 1) * DIFF_HEAD_DIM)
            q_m = jnp.where(in_map, q, jnp.zeros_like(q))
            q_cat = jnp.concatenate([q_m.astype(F32), q_augs[g]], axis=1)
            q_maps[id(t), i] = q_cat.T.astype(BF16)
        rows = pl.ds(pl.multiple_of(chunk_of(t, c) * KEY_CHUNK, KEY_CHUNK), KEY_CHUNK)
        return _dot(kaug_ref[t, g, rows, :], q_maps[id(t), i])

    def values(t, i, c):
        g = i // 2
        vt = vt_ref[0, chunk_of(t, c), g * HEAD_LANES:(g + 1) * HEAD_LANES, :]
        return jnp.concatenate([vt, ones], axis=0)

    def diagonal_bias(i, c, r):
        return dbias_ref[i // 2, r:r + SWEEP_ROWS, :] if c == 0 else None

    def finish(t, accs):
        d = HEAD_LANES
        for g in range(DIFF_GROUP):
            a0, a1 = accs[2 * g], accs[2 * g + 1]
            o_t = a0[:d] * (1.0 / a0[d:d + 1]) - a1[:d] * (lam / a1[d:d + 1])
            o = _rms(o_t.T, g_ref[...]) * (1.0 - lam_init)
            o_ref[0, tile_rows(t), g * d:(g + 1) * d] = o.astype(o_ref.dtype)

    _attend_tiles(n_tiles, 2 * DIFF_GROUP, n_keys, scores, values, finish, s_ref, e_ref,
                  diagonal_bias)


def _diff_attn(qk, vt, slopes, lq1, lk1, lq2, lk2, subln_g, lam_init):
    b, s, _ = qk.shape
    tq, g = ATTN_TQ, DIFF_GROUP
    n_groups = DIFF_HEADS // g
    n_chunks = s // KEY_CHUNK
    assert tq == KEY_CHUNK and vt.shape == (b, n_chunks, DIFF_HEADS * HEAD_LANES, KEY_CHUNK)
    kern = functools.partial(_diff_attn_kernel, lam_init=lam_init)
    vec = lambda a: _const_spec(a.shape)
    return pl.pallas_call(
        kern,
        grid=(n_groups, b),
        in_specs=[
            pl.BlockSpec((1, s, g * HEAD_LANES), lambda hi, bi: (bi, 0, hi)),
            pl.BlockSpec((1, s, g * HEAD_LANES), lambda hi, bi: (bi, 0, n_groups + hi)),
            pl.BlockSpec((1, n_chunks, g * HEAD_LANES, KEY_CHUNK), lambda hi, bi: (bi, 0, hi, 0)),
            pl.BlockSpec((g, 1, HEAD_LANES), lambda hi, bi: (hi, 0, 0)),
            vec(lq1), vec(lk1), vec(lq2), vec(lk2), vec(subln_g),
        ],
        out_specs=pl.BlockSpec((1, s, g * HEAD_LANES), lambda hi, bi: (bi, 0, hi)),
        out_shape=jax.ShapeDtypeStruct((b, s, DIFF_HEADS * HEAD_LANES), BF16),
        scratch_shapes=[pltpu.VMEM((s // tq, g, s, 2 * HEAD_LANES), BF16),
                        pltpu.VMEM((g, KEY_CHUNK, tq), F32),
                        pltpu.VMEM((2, s, tq), F32),
                        pltpu.VMEM((2, KEY_CHUNK, tq), BF16)],
        compiler_params=_params(2),
        name="diff_attn",
    )(qk, qk, vt, slopes, lq1, lk1, lq2, lk2, subln_g)


def _post_kernel(*refs, n_mix, final_norm):
    h_ref = refs[0]
    mix_refs = refs[1:1 + n_mix]
    wo_ref, ng_ref, wg_ref, wu_ref, wd_ref = refs[1 + n_mix:6 + n_mix]
    rest = refs[6 + n_mix:]
    fg_ref = rest[0] if final_norm else None
    o_ref = rest[-1]

    h = h_ref[...]
    row0 = 0
    for m_ref in mix_refs:
        width = m_ref.shape[1]
        h = h + _dot(m_ref[...], wo_ref[row0:row0 + width, :])
        row0 += width
    hn = _rms(h, ng_ref[...]).astype(BF16)
    gate = _dot(hn, wg_ref[...])
    up = _dot(hn, wu_ref[...])
    act = gate * jax.nn.sigmoid(gate) * up
    h = h + _dot(act.astype(BF16), wd_ref[...])
    if final_norm:
        h = _rms(h, fg_ref[...])
    o_ref[...] = h


def _post(h, mixes, w_out, norm_g, ffn_weights, layer, final_g, tm):
    n, d = h.shape
    n_mix = len(mixes)
    final_norm = final_g is not None
    kern = functools.partial(_post_kernel, n_mix=n_mix, final_norm=final_norm)
    row_spec = lambda a: pl.BlockSpec((tm, a.shape[1]), lambda i: (i, 0))
    layer_spec = lambda a: pl.BlockSpec((None,) + a.shape[1:], lambda i: (layer, 0, 0),
                                        pipeline_mode=pl.Buffered(1))
    consts = [w_out, norm_g] + list(ffn_weights) + ([final_g] if final_norm else [])
    const_specs = ([_const_spec(w_out.shape), _const_spec(norm_g.shape)]
                   + [layer_spec(w) for w in ffn_weights]
                   + ([_const_spec(final_g.shape)] if final_norm else []))
    return pl.pallas_call(
        kern,
        grid=(n // tm,),
        in_specs=[row_spec(h)] + [row_spec(m) for m in mixes] + const_specs,
        out_specs=pl.BlockSpec((tm, d), lambda i: (i, 0)),
        out_shape=jax.ShapeDtypeStruct((n, d), F32),
        compiler_params=_params(1),
        name="post_final" if final_norm else "post",
    )(h, *mixes, *consts)


def _rope_head_blocks(w_rope, lead):
    half = MLA_ROPE // 2
    tail = HEAD_LANES - lead - MLA_ROPE
    widths = [(0, 0)] * (w_rope.ndim - 1) + [(lead, tail)]
    swapped = jnp.concatenate([w_rope[..., half:], w_rope[..., :half]], axis=-1)
    return jnp.pad(w_rope, widths), jnp.pad(swapped, widths)


def _hybrid_weights(w_in, w_uq, w_ukv):
    o1 = POOL_WIDTH + MLA_Q_RANK + MLA_KV_RANK
    k_r, k_r_sw = _rope_head_blocks(w_in[:, o1:], MLA_NOPE)
    w_in_p = jnp.concatenate([w_in[:, :o1], k_r, k_r_sw], axis=1).astype(BF16)
    r = w_uq.shape[0]
    wq = w_uq.reshape(r, MLA_HEADS, MLA_NOPE + MLA_ROPE)
    q_r, q_r_sw = _rope_head_blocks(wq[:, :, MLA_NOPE:], MLA_NOPE)
    nope = jnp.pad(wq[:, :, :MLA_NOPE], ((0, 0), (0, 0), (0, HEAD_LANES - MLA_NOPE)))
    wq = jnp.concatenate([(nope + q_r).reshape(r, -1), q_r_sw.reshape(r, -1)],
                         axis=1).astype(BF16)
    wkv = w_ukv.reshape(r, MLA_HEADS, MLA_NOPE + MLA_V)
    wk = jnp.pad(wkv[:, :, :MLA_NOPE], ((0, 0), (0, 0), (0, HEAD_LANES - MLA_NOPE)))
    wk = wk.reshape(r, MLA_HEADS * HEAD_LANES).astype(BF16)
    wvt = wkv[:, :, MLA_NOPE:].reshape(r, MLA_HEADS * MLA_V).T.astype(BF16)
    return w_in_p, wq, wk, wvt


def _rope_tables(seq, scale):
    half = MLA_ROPE // 2
    inv = ROPE_THETA ** (-jnp.arange(0, MLA_ROPE, 2, dtype=F32) / MLA_ROPE)
    ang = jnp.arange(seq, dtype=F32)[:, None] * inv[None, :]
    cos, sin = jnp.cos(ang), jnp.sin(ang)
    z = lambda w: jnp.zeros((seq, w), F32)
    tail = HEAD_LANES - MLA_NOPE - MLA_ROPE
    c = jnp.concatenate([jnp.ones((seq, MLA_NOPE), F32), cos, cos, z(tail)], axis=1)
    sn = jnp.concatenate([z(MLA_NOPE), -sin, sin, z(tail)], axis=1)
    return c * scale, sn * scale


def kernel(x, attn_norm_g, ffn_norm_g, final_norm_g, hyb_w_in, pool_w, pool_scale,
           mla_q_norm_g, mla_w_uq, mla_kv_norm_g, mla_w_ukv, hyb_w_out,
           diff_w_qkv, diff_lambda_q1, diff_lambda_k1, diff_lambda_q2, diff_lambda_k2,
           diff_subln_g, diff_w_out, ffn_w_gate, ffn_w_up, ffn_w_down):
    b, s, d = x.shape
    n = b * s
    row = lambda a: a.reshape(1, -1)
    h = x.reshape(n, d)

    w_in_p, wq_p, wk_p, wvt_p = _hybrid_weights(hyb_w_in[0], mla_w_uq[0], mla_w_ukv[0])
    q_tabs = _rope_tables(s, float((MLA_NOPE + MLA_ROPE) ** -0.5 * LOG2E))
    k_tabs = _rope_tables(s, 1.0)
    a_out, q, k, vt = _mla_prep(
        x, row(attn_norm_g[0]), w_in_p, pool_w[0].astype(BF16), row(pool_scale[0]),
        row(mla_q_norm_g[0]), wq_p, row(mla_kv_norm_g[0]), wk_p, wvt_p, q_tabs, k_tabs, 1024)
    b_out = _mla_attn(q, k, vt)
    ffn_weights = (ffn_w_gate.astype(BF16), ffn_w_up.astype(BF16), ffn_w_down.astype(BF16))
    h = _post(h, [a_out.reshape(n, -1), b_out.reshape(n, -1)], hyb_w_out[0].astype(BF16),
              row(ffn_norm_g[0]), ffn_weights, 0, None, 512)

    qk_w = DIFF_HEADS * 2 * DIFF_HEAD_DIM
    w_qkv = diff_w_qkv[0]
    col_scale = jnp.where(jnp.arange(2 * qk_w) < qk_w, DIFF_HEAD_DIM ** -0.5 * LOG2E, 1.0)
    w_qk = (w_qkv[:, :2 * qk_w] * col_scale.astype(F32)).astype(BF16)
    w_vt = w_qkv[:, 2 * qk_w:].astype(BF16).T
    qk, vt = _norm_qkv(h.reshape(b, s, d), row(attn_norm_g[1]), w_qk, w_vt, KEY_CHUNK)
    slopes = 2.0 ** (-8.0 * jnp.arange(1, DIFF_HEADS + 1, dtype=F32) / DIFF_HEADS)
    slopes2 = jnp.broadcast_to((LOG2E * slopes)[:, None, None], (DIFF_HEADS, 1, HEAD_LANES))
    lam_init = 0.8 - 0.6 * math.exp(-0.3 * 1)
    o = _diff_attn(qk, vt, slopes2, row(diff_lambda_q1[0]),
                   row(diff_lambda_k1[0]), row(diff_lambda_q2[0]), row(diff_lambda_k2[0]),
                   row(diff_subln_g[0]), lam_init)
    h = _post(h, [o.reshape(n, -1)], diff_w_out[0].astype(BF16),
              row(ffn_norm_g[1]), ffn_weights, 1, row(final_norm_g), 512)
    return h.reshape(b, s, d)
```

```python
import functools
import math

import jax
import jax.numpy as jnp
from jax import lax
from jax.experimental import pallas as pl
from jax.experimental.pallas import tpu as pltpu

D_MODEL = 1024
EPS = 1e-6
LOG2E = math.log2(math.e)

POOL_WINDOWS = (2, 4, 8, 16)
POOL_GROUP_DIM = 128
POOL_WIDTH = 512
POOL_HALO = 8

MLA_HEADS = 8
MLA_NOPE = 64
MLA_ROPE = 32
MLA_V = 64
MLA_Q_RANK = 256
MLA_KV_RANK = 256
ROPE_THETA = 10000.0
HEAD_LANES = 128
PROJ_WIDTH = POOL_WIDTH + MLA_Q_RANK + MLA_KV_RANK + 2 * HEAD_LANES

DIFF_HEADS = 8
DIFF_HEAD_DIM = 64
D_FF = 2816

SUBLANES = 8
BF16_ROWS = 16
KEY_CHUNK = 512
ATTN_TQ = 512
MLA_GROUP = 8
DIFF_GROUP = 4

VMEM_LIMIT = 56 * 1024 * 1024

F32 = jnp.float32
BF16 = jnp.bfloat16


def _rms(x, g):
    return x * lax.rsqrt(jnp.mean(x * x, axis=-1, keepdims=True) + EPS) * g


def _dot(a, b):
    return jnp.dot(a, b, preferred_element_type=F32)


def _dot_nt(a, b):
    return lax.dot_general(a, b, (((1,), (1,)), ((), ())), preferred_element_type=F32)


def _const_spec(shape):
    nd = len(shape)
    return pl.BlockSpec(shape, lambda *_: (0,) * nd, pipeline_mode=pl.Buffered(1))


def _params(n_axes):
    return pltpu.CompilerParams(dimension_semantics=("arbitrary",) * n_axes,
                                vmem_limit_bytes=VMEM_LIMIT)


def _norm_qkv_kernel(x_ref, g_ref, wqk_ref, wvt_ref, qk_ref, vt_ref):
    y = _rms(x_ref[0], g_ref[...]).astype(BF16)
    qk_ref[0] = _dot(y, wqk_ref[...]).astype(qk_ref.dtype)
    vt_ref[0, 0] = _dot_nt(wvt_ref[...], y).astype(vt_ref.dtype)


def _norm_qkv(x, g, w_qk, w_vt, tm):
    b, s, d = x.shape
    return pl.pallas_call(
        _norm_qkv_kernel,
        grid=(b, s // tm),
        in_specs=[pl.BlockSpec((1, tm, d), lambda bi, i: (bi, i, 0)),
                  _const_spec(g.shape), _const_spec(w_qk.shape), _const_spec(w_vt.shape)],
        out_specs=[pl.BlockSpec((1, tm, w_qk.shape[1]), lambda bi, i: (bi, i, 0)),
                   pl.BlockSpec((1, 1, w_vt.shape[0], tm), lambda bi, i: (bi, i, 0, 0))],
        out_shape=[jax.ShapeDtypeStruct((b, s, w_qk.shape[1]), BF16),
                   jax.ShapeDtypeStruct((b, s // tm, w_vt.shape[0], tm), BF16)],
        compiler_params=_params(2),
        name="norm_qkv",
    )(x, g, w_qk, w_vt)


def _mla_prep_kernel(x_ref, xprev_ref, xnext_ref, ng_ref, win_ref, pool_w_ref, pool_scale_ref,
                     qg_ref, wuq_ref, kvg_ref, wuk_ref, wuvt_ref,
                     qc_ref, qs_ref, kc_ref, ks_ref,
                     a_ref, q_ref, k_ref, vt_ref, cur_ref, pad_ref, *, ts, seq):
    i = pl.program_id(1)
    n_tiles = pl.num_programs(1)

    cur_ref[0] = _dot(_rms(x_ref[0], ng_ref[...]).astype(BF16), win_ref[...])
    x_halo = jnp.concatenate([xprev_ref[0], xnext_ref[0]], axis=0)
    u_halo = _dot(_rms(x_halo, ng_ref[...]).astype(BF16), win_ref[:, 0:POOL_WIDTH])

    pad_ref[0:POOL_HALO, :] = jnp.where(i > 0, u_halo[:POOL_HALO], 0.0)
    pad_ref[POOL_HALO:POOL_HALO + ts, :] = cur_ref[0, :, 0:POOL_WIDTH]
    pad_ref[POOL_HALO + ts:2 * POOL_HALO + ts, :] = jnp.where(i < n_tiles - 1,
                                                              u_halo[POOL_HALO:], 0.0)
    t = i * ts + lax.broadcasted_iota(jnp.int32, (ts, 1), 0)
    for g, w in enumerate(POOL_WINDOWS):
        lanes = slice(g * POOL_GROUP_DIM, (g + 1) * POOL_GROUP_DIM)
        acc = None
        for off in range(-(w // 2), w // 2):
            piece = pad_ref[POOL_HALO + off:POOL_HALO + off + ts, lanes]
            acc = piece if acc is None else acc + piece
        cnt = jnp.minimum(t + w // 2, seq) - jnp.maximum(t - w // 2, 0)
        inv_cnt = 1.0 / cnt.astype(F32)
        pooled = acc * inv_cnt - cur_ref[0, :, lanes]
        a_g = _dot(pooled.astype(BF16), pool_w_ref[g]) * pool_scale_ref[:, lanes]
        a_ref[0, :, lanes] = a_g.astype(a_ref.dtype)

    c_q = cur_ref[0, :, POOL_WIDTH:POOL_WIDTH + MLA_Q_RANK]
    q = _dot(_rms(c_q, qg_ref[...]).astype(BF16), wuq_ref[...])
    qc, qs = qc_ref[...], qs_ref[...]
    partner = MLA_HEADS * HEAD_LANES
    for h in range(MLA_HEADS):
        lanes = slice(h * HEAD_LANES, (h + 1) * HEAD_LANES)
        lanes_p = slice(partner + h * HEAD_LANES, partner + (h + 1) * HEAD_LANES)
        q_ref[0, :, lanes] = (q[:, lanes] * qc + q[:, lanes_p] * qs).astype(q_ref.dtype)

    o_kv = POOL_WIDTH + MLA_Q_RANK
    c_kv = cur_ref[0, :, o_kv:o_kv + MLA_KV_RANK]
    kvn = _rms(c_kv, kvg_ref[...]).astype(BF16)
    k_nope = _dot(kvn, wuk_ref[...])
    o_kr = o_kv + MLA_KV_RANK
    k_rope = (cur_ref[0, :, o_kr:o_kr + HEAD_LANES] * kc_ref[...]
              + cur_ref[0, :, o_kr + HEAD_LANES:o_kr + 2 * HEAD_LANES] * ks_ref[...])
    for h in range(MLA_HEADS):
        lanes = slice(h * HEAD_LANES, (h + 1) * HEAD_LANES)
        k_ref[0, :, lanes] = (k_nope[:, lanes] + k_rope).astype(k_ref.dtype)
    vt_ref[0] = _dot_nt(wuvt_ref[...], kvn).astype(vt_ref.dtype)


def _mla_prep(x, norm_g, w_in, pool_w, pool_scale, qg, wuq, kvg, wuk, wuvt, q_tabs, k_tabs, ts):
    b, s, d = x.shape
    hb = ts // POOL_HALO
    n_halo = s // POOL_HALO
    tab_spec = pl.BlockSpec((ts, HEAD_LANES), lambda bi, i: (i, 0))
    kern = functools.partial(_mla_prep_kernel, ts=ts, seq=s)
    consts = [norm_g, w_in, pool_w, pool_scale, qg, wuq, kvg, wuk, wuvt]
    row_blk = lambda w: pl.BlockSpec((1, ts, w), lambda bi, i: (bi, i, 0))
    return pl.pallas_call(
        kern,
        grid=(b, s // ts),
        in_specs=[
            row_blk(d),
            pl.BlockSpec((1, POOL_HALO, d), lambda bi, i: (bi, jnp.maximum(i * hb - 1, 0), 0)),
            pl.BlockSpec((1, POOL_HALO, d),
                         lambda bi, i: (bi, jnp.minimum((i + 1) * hb, n_halo - 1), 0)),
        ] + [_const_spec(c.shape) for c in consts] + [tab_spec] * 4,
        out_specs=[
            row_blk(POOL_WIDTH),
            row_blk(MLA_HEADS * HEAD_LANES),
            row_blk(MLA_HEADS * HEAD_LANES),
            pl.BlockSpec((1, MLA_HEADS * MLA_V, ts), lambda bi, i: (bi, 0, i)),
        ],
        out_shape=[
            jax.ShapeDtypeStruct((b, s, POOL_WIDTH), BF16),
            jax.ShapeDtypeStruct((b, s, MLA_HEADS * HEAD_LANES), BF16),
            jax.ShapeDtypeStruct((b, s, MLA_HEADS * HEAD_LANES), BF16),
            jax.ShapeDtypeStruct((b, MLA_HEADS * MLA_V, s), BF16),
        ],
        scratch_shapes=[pltpu.VMEM((1, ts, PROJ_WIDTH), F32),
                        pltpu.VMEM((ts + 2 * POOL_HALO, POOL_WIDTH), F32)],
        compiler_params=_params(2),
        name="mla_prep",
    )(x, x, x, *consts, *q_tabs, *k_tabs)


def _ones_rows(cols):
    row = lax.broadcasted_iota(jnp.int32, (BF16_ROWS, cols), 0)
    return jnp.where(row == 0, 1.0, 0.0).astype(BF16)


SWEEP_ROWS = 16


def _attend_tiles(n_tiles, n_items, n_keys, scores, values, finish, s_ref, e_ref, extra=None,
                  extra_ahead=1):
    n_chunks = n_keys // KEY_CHUNK
    tq = s_ref.shape[-1]
    assert n_items % 2 == 0 and n_items >= 2 and n_chunks % 2 == 0

    def park(t, i, c):
        s_ref[i % 2, c * KEY_CHUNK:(c + 1) * KEY_CHUNK, :] = scores(t, i, c)

    def logits(i, c, r):
        blk = s_ref[i % 2, c * KEY_CHUNK + r:c * KEY_CHUNK + r + SWEEP_ROWS, :]
        add = None if extra is None else extra(i % n_items, c, r)
        return blk if add is None else blk + add

    def block_max(i, c, r, m8):
        blk = logits(i, c, r).reshape(SWEEP_ROWS // SUBLANES, SUBLANES, tq)
        for part in range(SWEEP_ROWS // SUBLANES):
            m8 = blk[part] if m8 is None else jnp.maximum(m8, blk[part])
        return m8

    m8 = None
    for c in range(n_chunks):
        park(0, 0, c)
    for c in range(n_chunks):
        for r in range(0, KEY_CHUNK, SWEEP_ROWS):
            m8 = block_max(0, c, r, m8)
    for c in range(extra_ahead):
        park(0, 1, c)

    def tile(t, m8):
        t_next = jnp.minimum(t + 1, n_tiles - 1)
        accs = [None] * n_items

        def value_matmul(pos):
            i, c = divmod(pos, n_chunks)
            pv = _dot(values(t, i, c), e_ref[pos % 2])
            accs[i] = pv if accs[i] is None else accs[i] + pv

        for i in range(n_items):
            m = jnp.broadcast_to(jnp.max(m8, axis=0, keepdims=True), (SWEEP_ROWS, tq))
            m8 = None
            for c in range(n_chunks):
                pos = i * n_chunks + c
                ahead_i, ahead_c = divmod(pos + n_chunks + extra_ahead, n_chunks)
                if ahead_i < n_items:
                    park(t, ahead_i, ahead_c)
                else:
                    park(t_next, ahead_i - n_items, ahead_c)
                for r in range(0, KEY_CHUNK, SWEEP_ROWS):
                    e_ref[pos % 2, r:r + SWEEP_ROWS, :] = (
                        jnp.exp2(logits(i, c, r) - m).astype(BF16))
                    m8 = block_max(i + 1, c, r, m8)
                if pos > 0:
                    value_matmul(pos - 1)
        value_matmul(n_items * n_chunks - 1)
        finish(t, accs)
        return m8

    lax.fori_loop(0, n_tiles, tile, m8)


def _mla_attn_kernel(q_ref, k_ref, vt_ref, o_ref, s_ref, e_ref):
    tq = s_ref.shape[-1]
    ones = _ones_rows(KEY_CHUNK)

    def tile_rows(t):
        return pl.ds(pl.multiple_of(t * tq, tq), tq)

    q_t = {}

    def scores(t, j, c):
        lanes = slice(j * HEAD_LANES, (j + 1) * HEAD_LANES)
        keys = slice(c * KEY_CHUNK, (c + 1) * KEY_CHUNK)
        if (id(t), j) not in q_t:
            q_t[id(t), j] = q_ref[0, tile_rows(t), lanes].T
        return _dot(k_ref[0, keys, lanes], q_t[id(t), j])

    def values(t, j, c):
        vt = vt_ref[0, j * MLA_V:(j + 1) * MLA_V, c * KEY_CHUNK:(c + 1) * KEY_CHUNK]
        return jnp.concatenate([vt, ones], axis=0)

    def finish(t, accs):
        outs = [a[:MLA_V] * (1.0 / a[MLA_V:MLA_V + 1]) for a in accs]
        o_ref[0, tile_rows(t), :] = jnp.concatenate(outs, axis=0).T.astype(o_ref.dtype)

    _attend_tiles(q_ref.shape[1] // tq, MLA_GROUP, k_ref.shape[1], scores, values, finish,
                  s_ref, e_ref)


def _mla_attn(q, k, vt):
    b, s, _ = q.shape
    tq, g = ATTN_TQ, MLA_GROUP
    return pl.pallas_call(
        _mla_attn_kernel,
        grid=(b, MLA_HEADS // g),
        in_specs=[
            pl.BlockSpec((1, s, g * HEAD_LANES), lambda bi, p: (bi, 0, p)),
            pl.BlockSpec((1, s, g * HEAD_LANES), lambda bi, p: (bi, 0, p)),
            pl.BlockSpec((1, g * MLA_V, s), lambda bi, p: (bi, p, 0)),
        ],
        out_specs=pl.BlockSpec((1, s, g * MLA_V), lambda bi, p: (bi, 0, p)),
        out_shape=jax.ShapeDtypeStruct((b, s, MLA_HEADS * MLA_V), BF16),
        scratch_shapes=[pltpu.VMEM((2, s, tq), F32), pltpu.VMEM((2, KEY_CHUNK, tq), BF16)],
        compiler_params=_params(2),
        name="mla_attn",
    )(q, k, vt)


def _bf16_pieces(x):
    p1 = x.astype(BF16).astype(F32)
    p2 = (x - p1).astype(BF16).astype(F32)
    p3 = (x - p1 - p2).astype(BF16).astype(F32)
    return p1, p2, p3


def _piece_lanes(lane, pieces):
    sel = lane % 3
    return jnp.where(sel == 0, pieces[0], jnp.where(sel == 1, pieces[1], pieces[2]))


def _diff_attn_kernel(q_ref, k_ref, vt_ref, slope_ref, lq1_ref, lk1_ref, lq2_ref, lk2_ref,
                      g_ref, o_ref, kaug_ref, dbias_ref, s_ref, e_ref, *, lam_init):
    bi = pl.program_id(1)
    tq = s_ref.shape[-1]
    n_keys = k_ref.shape[1]
    n_chunks = n_keys // KEY_CHUNK
    n_tiles = q_ref.shape[1] // tq
    half = KEY_CHUNK // 2
    aug = HEAD_LANES

    slopes = [slope_ref[g, :, 0:1] for g in range(DIFF_GROUP)]
    pieces = [_bf16_pieces(c) for c in slopes]

    @pl.when(bi == 0)
    def _():
        j = lax.broadcasted_iota(jnp.int32, (n_keys, aug), 0)
        lane = lax.broadcasted_iota(jnp.int32, (n_keys, aug), 1)
        chunk = j // KEY_CHUNK
        dj = j % KEY_CHUNK
        lo = (dj % half).astype(F32)
        dk = lax.broadcasted_iota(jnp.int32, (KEY_CHUNK, tq), 0)
        dq = lax.broadcasted_iota(jnp.int32, (KEY_CHUNK, tq), 1)
        dist = jnp.abs(dq - dk).astype(F32)
        for g in range(DIFF_GROUP):
            neg_c = -_piece_lanes(lane, pieces[g])
            dbias_ref[g] = -slopes[g] * dist
            for qi in range(n_tiles):
                side = jnp.sign(qi - chunk).astype(F32)
                hi = (dj - dj % half - (qi - chunk) * KEY_CHUNK).astype(F32)
                val = jnp.where(lane < 6, neg_c,
                                jnp.where(lane < 9, lo, jnp.where(lane < 12, hi, 0.0)))
                kaug_ref[qi, g, :, aug:] = (side * val).astype(BF16)

    lam = (jnp.exp(jnp.sum(lq1_ref[...] * lk1_ref[...], axis=-1, keepdims=True))
           - jnp.exp(jnp.sum(lq2_ref[...] * lk2_ref[...], axis=-1, keepdims=True))
           + lam_init)

    lane = lax.broadcasted_iota(jnp.int32, (tq, HEAD_LANES), 1)
    di = lax.broadcasted_iota(jnp.int32, (tq, aug), 0)
    q_lo = (di % half).astype(F32)
    q_hi = (di - di % half).astype(F32)
    q_augs = []
    for g in range(DIFF_GROUP):
        k = k_ref[0, :, g * HEAD_LANES:(g + 1) * HEAD_LANES]
        for qi in range(n_tiles):
            kaug_ref[qi, g, :, :aug] = k
        c_lanes = _piece_lanes(lane, pieces[g])
        q_augs.append(jnp.where(lane < 3, q_lo, jnp.where(lane < 6, q_hi,
                                                           jnp.where(lane < 12, c_lanes, 0.0))))
    ones = _ones_rows(KEY_CHUNK)
    q_maps = {}

    def tile_rows(t):
        return pl.ds(pl.multiple_of(t * tq, tq), tq)

    def chunk_of(t, c):
        return (t + c) % n_chunks

    def scores(t, i, c):
        g, m = divmod(i, 2)
        if (id(t), i) not in q_maps:
            q = q_ref[0, tile_rows(t), g * HEAD_LANES:(g + 1) * HEAD_LANES]
            in_map = (lane >= m * DIFF_HEAD_DIM) & (lane < (m + 1) * DIFF_HEAD_DIM)
            q_m = jnp.where(in_map, q, jnp.zeros_like(q))
            q_cat = jnp.concatenate([q_m.astype(F32), q_augs[g]], axis=1)
            q_maps[id(t), i] = q_cat.T.astype(BF16)
        rows = pl.ds(pl.multiple_of(chunk_of(t, c) * KEY_CHUNK, KEY_CHUNK), KEY_CHUNK)
        return _dot(kaug_ref[t, g, rows, :], q_maps[id(t), i])

    def values(t, i, c):
        g = i // 2
        vt = vt_ref[0, chunk_of(t, c), g * HEAD_LANES:(g + 1) * HEAD_LANES, :]
        return jnp.concatenate([vt, ones], axis=0)

    def diagonal_bias(i, c, r):
        return dbias_ref[i // 2, r:r + SWEEP_ROWS, :] if c == 0 else None

    def finish(t, accs):
        d = HEAD_LANES
        for g in range(DIFF_GROUP):
            a0, a1 = accs[2 * g], accs[2 * g + 1]
            o_t = a0[:d] * (1.0 / a0[d:d + 1]) - a1[:d] * (lam / a1[d:d + 1])
            o = _rms(o_t.T, g_ref[...]) * (1.0 - lam_init)
            o_ref[0, tile_rows(t), g * d:(g + 1) * d] = o.astype(o_ref.dtype)

    _attend_tiles(n_tiles, 2 * DIFF_GROUP, n_keys, scores, values, finish, s_ref, e_ref,
                  diagonal_bias, extra_ahead=0)


def _diff_attn(qk, vt, slopes, lq1, lk1, lq2, lk2, subln_g, lam_init):
    b, s, _ = qk.shape
    tq, g = ATTN_TQ, DIFF_GROUP
    n_groups = DIFF_HEADS // g
    n_chunks = s // KEY_CHUNK
    assert tq == KEY_CHUNK and vt.shape == (b, n_chunks, DIFF_HEADS * HEAD_LANES, KEY_CHUNK)
    kern = functools.partial(_diff_attn_kernel, lam_init=lam_init)
    vec = lambda a: _const_spec(a.shape)
    return pl.pallas_call(
        kern,
        grid=(n_groups, b),
        in_specs=[
            pl.BlockSpec((1, s, g * HEAD_LANES), lambda hi, bi: (bi, 0, hi)),
            pl.BlockSpec((1, s, g * HEAD_LANES), lambda hi, bi: (bi, 0, n_groups + hi)),
            pl.BlockSpec((1, n_chunks, g * HEAD_LANES, KEY_CHUNK), lambda hi, bi: (bi, 0, hi, 0)),
            pl.BlockSpec((g, 1, HEAD_LANES), lambda hi, bi: (hi, 0, 0)),
            vec(lq1), vec(lk1), vec(lq2), vec(lk2), vec(subln_g),
        ],
        out_specs=pl.BlockSpec((1, s, g * HEAD_LANES), lambda hi, bi: (bi, 0, hi)),
        out_shape=jax.ShapeDtypeStruct((b, s, DIFF_HEADS * HEAD_LANES), BF16),
        scratch_shapes=[pltpu.VMEM((s // tq, g, s, 2 * HEAD_LANES), BF16),
                        pltpu.VMEM((g, KEY_CHUNK, tq), F32),
                        pltpu.VMEM((2, s, tq), F32),
                        pltpu.VMEM((2, KEY_CHUNK, tq), BF16)],
        compiler_params=_params(2),
        name="diff_attn",
    )(qk, qk, vt, slopes, lq1, lk1, lq2, lk2, subln_g)


def _post_kernel(*refs, n_mix, final_norm):
    h_ref = refs[0]
    mix_refs = refs[1:1 + n_mix]
    wo_ref, ng_ref, wg_ref, wu_ref, wd_ref = refs[1 + n_mix:6 + n_mix]
    rest = refs[6 + n_mix:]
    fg_ref = rest[0] if final_norm else None
    o_ref = rest[-1]

    h = h_ref[...]
    row0 = 0
    for m_ref in mix_refs:
        width = m_ref.shape[1]
        h = h + _dot(m_ref[...], wo_ref[row0:row0 + width, :])
        row0 += width
    hn = _rms(h, ng_ref[...]).astype(BF16)
    gate = _dot(hn, wg_ref[...])
    up = _dot(hn, wu_ref[...])
    act = gate * jax.nn.sigmoid(gate) * up
    h = h + _dot(act.astype(BF16), wd_ref[...])
    if final_norm:
        h = _rms(h, fg_ref[...])
    o_ref[...] = h


def _post(h, mixes, w_out, norm_g, ffn_weights, layer, final_g, tm):
    n, d = h.shape
    n_mix = len(mixes)
    final_norm = final_g is not None
    kern = functools.partial(_post_kernel, n_mix=n_mix, final_norm=final_norm)
    row_spec = lambda a: pl.BlockSpec((tm, a.shape[1]), lambda i: (i, 0))
    layer_spec = lambda a: pl.BlockSpec((None,) + a.shape[1:], lambda i: (layer, 0, 0),
                                        pipeline_mode=pl.Buffered(1))
    consts = [w_out, norm_g] + list(ffn_weights) + ([final_g] if final_norm else [])
    const_specs = ([_const_spec(w_out.shape), _const_spec(norm_g.shape)]
                   + [layer_spec(w) for w in ffn_weights]
                   + ([_const_spec(final_g.shape)] if final_norm else []))
    return pl.pallas_call(
        kern,
        grid=(n // tm,),
        in_specs=[row_spec(h)] + [row_spec(m) for m in mixes] + const_specs,
        out_specs=pl.BlockSpec((tm, d), lambda i: (i, 0)),
        out_shape=jax.ShapeDtypeStruct((n, d), F32),
        compiler_params=_params(1),
        name="post_final" if final_norm else "post",
    )(h, *mixes, *consts)


def _rope_head_blocks(w_rope, lead):
    half = MLA_ROPE // 2
    tail = HEAD_LANES - lead - MLA_ROPE
    widths = [(0, 0)] * (w_rope.ndim - 1) + [(lead, tail)]
    swapped = jnp.concatenate([w_rope[..., half:], w_rope[..., :half]], axis=-1)
    return jnp.pad(w_rope, widths), jnp.pad(swapped, widths)


def _hybrid_weights(w_in, w_uq, w_ukv):
    o1 = POOL_WIDTH + MLA_Q_RANK + MLA_KV_RANK
    k_r, k_r_sw = _rope_head_blocks(w_in[:, o1:], MLA_NOPE)
    w_in_p = jnp.concatenate([w_in[:, :o1], k_r, k_r_sw], axis=1).astype(BF16)
    r = w_uq.shape[0]
    wq = w_uq.reshape(r, MLA_HEADS, MLA_NOPE + MLA_ROPE)
    q_r, q_r_sw = _rope_head_blocks(wq[:, :, MLA_NOPE:], MLA_NOPE)
    nope = jnp.pad(wq[:, :, :MLA_NOPE], ((0, 0), (0, 0), (0, HEAD_LANES - MLA_NOPE)))
    wq = jnp.concatenate([(nope + q_r).reshape(r, -1), q_r_sw.reshape(r, -1)],
                         axis=1).astype(BF16)
    wkv = w_ukv.reshape(r, MLA_HEADS, MLA_NOPE + MLA_V)
    wk = jnp.pad(wkv[:, :, :MLA_NOPE], ((0, 0), (0, 0), (0, HEAD_LANES - MLA_NOPE)))
    wk = wk.reshape(r, MLA_HEADS * HEAD_LANES).astype(BF16)
    wvt = wkv[:, :, MLA_NOPE:].reshape(r, MLA_HEADS * MLA_V).T.astype(BF16)
    return w_in_p, wq, wk, wvt


def _rope_tables(seq, scale):
    half = MLA_ROPE // 2
    inv = ROPE_THETA ** (-jnp.arange(0, MLA_ROPE, 2, dtype=F32) / MLA_ROPE)
    ang = jnp.arange(seq, dtype=F32)[:, None] * inv[None, :]
    cos, sin = jnp.cos(ang), jnp.sin(ang)
    z = lambda w: jnp.zeros((seq, w), F32)
    tail = HEAD_LANES - MLA_NOPE - MLA_ROPE
    c = jnp.concatenate([jnp.ones((seq, MLA_NOPE), F32), cos, cos, z(tail)], axis=1)
    sn = jnp.concatenate([z(MLA_NOPE), -sin, sin, z(tail)], axis=1)
    return c * scale, sn * scale


def kernel(x, attn_norm_g, ffn_norm_g, final_norm_g, hyb_w_in, pool_w, pool_scale,
           mla_q_norm_g, mla_w_uq, mla_kv_norm_g, mla_w_ukv, hyb_w_out,
           diff_w_qkv, diff_lambda_q1, diff_lambda_k1, diff_lambda_q2, diff_lambda_k2,
           diff_subln_g, diff_w_out, ffn_w_gate, ffn_w_up, ffn_w_down):
    b, s, d = x.shape
    n = b * s
    row = lambda a: a.reshape(1, -1)
    h = x.reshape(n, d)

    w_in_p, wq_p, wk_p, wvt_p = _hybrid_weights(hyb_w_in[0], mla_w_uq[0], mla_w_ukv[0])
    q_tabs = _rope_tables(s, float((MLA_NOPE + MLA_ROPE) ** -0.5 * LOG2E))
    k_tabs = _rope_tables(s, 1.0)
    a_out, q, k, vt = _mla_prep(
        x, row(attn_norm_g[0]), w_in_p, pool_w[0].astype(BF16), row(pool_scale[0]),
        row(mla_q_norm_g[0]), wq_p, row(mla_kv_norm_g[0]), wk_p, wvt_p, q_tabs, k_tabs, 1024)
    b_out = _mla_attn(q, k, vt)
    ffn_weights = (ffn_w_gate.astype(BF16), ffn_w_up.astype(BF16), ffn_w_down.astype(BF16))
    h = _post(h, [a_out.reshape(n, -1), b_out.reshape(n, -1)], hyb_w_out[0].astype(BF16),
              row(ffn_norm_g[0]), ffn_weights, 0, None, 512)

    qk_w = DIFF_HEADS * 2 * DIFF_HEAD_DIM
    w_qkv = diff_w_qkv[0]
    col_scale = jnp.where(jnp.arange(2 * qk_w) < qk_w, DIFF_HEAD_DIM ** -0.5 * LOG2E, 1.0)
    w_qk = (w_qkv[:, :2 * qk_w] * col_scale.astype(F32)).astype(BF16)
    w_vt = w_qkv[:, 2 * qk_w:].astype(BF16).T
    qk, vt = _norm_qkv(h.reshape(b, s, d), row(attn_norm_g[1]), w_qk, w_vt, KEY_CHUNK)
    slopes = 2.0 ** (-8.0 * jnp.arange(1, DIFF_HEADS + 1, dtype=F32) / DIFF_HEADS)
    slopes2 = jnp.broadcast_to((LOG2E * slopes)[:, None, None], (DIFF_HEADS, 1, HEAD_LANES))
    lam_init = 0.8 - 0.6 * math.exp(-0.3 * 1)
    o = _diff_attn(qk, vt, slopes2, row(diff_lambda_q1[0]),
                   row(diff_lambda_k1[0]), row(diff_lambda_q2[0]), row(diff_lambda_k2[0]),
                   row(diff_subln_g[0]), lam_init)
    h = _post(h, [o.reshape(n, -1)], diff_w_out[0].astype(BF16),
              row(ffn_norm_g[1]), ffn_weights, 1, row(final_norm_g), 512)
    return h.reshape(b, s, d)
```
